```python
import functools
import jax
import jax.numpy as jnp
from jax import lax
import numpy as np

D_MODEL = 1024
BATCH = 2
SEQ = 8192
DEPTH = 2
DEC_BATCH = 32
DEC_SEQ = 4
PAST_LEN = 16384
PAGE_SIZE = 128

HEAD_DIM = 64
HEADS_PER_GROUP = 8
GROUPS = ((128, 1), (512, 4), (2048, 16))
N_GROUPS = 3
SPAN = 128
N_KEYS = SPAN + 1
ATTN_DIM = N_GROUPS * HEADS_PER_GROUP * HEAD_DIM
ATTN_OUT_DIM = HEADS_PER_GROUP * HEAD_DIM
CONV_DIM = D_MODEL
CONV_WIDTH = 3
D_FF = 2816
QBLOCK = 128
IN_DIM = 3 * ATTN_DIM + 3 * CONV_DIM + 2 * D_MODEL
RMS_EPS = 1e-6
FFN_RESIDUAL_WEIGHT = 0.5
ATTN_SCALE = HEAD_DIM ** -0.5
NEG = -1e30

kernel_name = "dilated_swa_shortconv_gated_hybrid_step"


def rmsnorm(x, g):
    xf = x.astype(jnp.float32)
    inv = lax.rsqrt(jnp.mean(xf * xf, axis=-1, keepdims=True) + RMS_EPS)
    return (xf * inv).astype(x.dtype) * g


def half_swiglu(x, g, w_in, w_out):
    h = rmsnorm(x, g) @ w_in
    gate, up = jnp.split(h, 2, axis=-1)
    return FFN_RESIDUAL_WEIGHT * ((jax.nn.silu(gate) * up) @ w_out)


def split_proj(u, w_in, b_gate):
    z = u @ w_in
    cuts = [ATTN_DIM, 2 * ATTN_DIM, 3 * ATTN_DIM, 3 * ATTN_DIM + CONV_DIM,
            3 * ATTN_DIM + 2 * CONV_DIM, 3 * ATTN_DIM + 3 * CONV_DIM]
    q, k, v, c_b, c_c, c_h, gates = jnp.split(z, cuts, axis=-1)
    heads = lambda a: a.reshape(a.shape[0], a.shape[1], N_GROUPS, HEADS_PER_GROUP, HEAD_DIM)
    g_attn, g_conv = jnp.split(jax.nn.sigmoid(gates + b_gate), 2, axis=-1)
    return heads(q), heads(k), heads(v), c_b, c_c, c_h, g_attn, g_conv


def dilated_attn_prompt(q, k, v, dil):
    b, s, h, e = q.shape
    L = s // dil
    nb = -(-L // QBLOCK)
    lp = nb * QBLOCK

    def by_stride(a):
        a = a.reshape(b, L, dil, h, e).transpose(0, 2, 1, 3, 4)
        a = jnp.pad(a, ((0, 0), (0, 0), (0, lp - L), (0, 0), (0, 0)))
        return a.reshape(b, dil, nb, QBLOCK, h, e)

    def with_prev(a):
        prev = jnp.pad(a[:, :, :-1], ((0, 0), (0, 0), (1, 0), (0, 0), (0, 0), (0, 0)))
        return jnp.concatenate([prev, a], axis=3)

    qs = by_stride(q)
    kb = with_prev(by_stride(k))
    vb = with_prev(by_stride(v))
    scores = jnp.einsum('brnqhe,brnkhe->brnhqk', qs, kb,
                        preferred_element_type=jnp.float32) * ATTN_SCALE
    qi = jnp.arange(QBLOCK)[:, None]
    ki = jnp.arange(2 * QBLOCK)[None, :]
    rel = QBLOCK + qi - ki
    key_u = jnp.arange(nb)[:, None, None] * QBLOCK - QBLOCK + ki[None]
    valid = (rel >= 0) & (rel <= SPAN) & (key_u >= 0)
    scores = jnp.where(valid[:, None], scores, NEG)
    lse = jax.nn.logsumexp(scores, axis=-1)
    p = jnp.exp(scores - lse[..., None])
    o = jnp.einsum('brnhqk,brnkhe->brnqhe', p, vb.astype(jnp.float32))
    o = o.reshape(b, dil, lp, h, e)[:, :, :L].transpose(0, 2, 1, 3, 4).reshape(b, s, h, e)
    lse = lse.transpose(0, 1, 2, 4, 3).reshape(b, dil, lp, h)[:, :, :L]
    lse = lse.transpose(0, 2, 1, 3).reshape(b, s, h)
    return o, lse


def dilated_attn_sample(q, k_new, v_new, kv_buf, dil, window):
    t = q.shape[1]
    lb = kv_buf.shape[1]
    kc = jnp.concatenate([kv_buf[:, :, 0], k_new], axis=1)
    vc = jnp.concatenate([kv_buf[:, :, 1], v_new], axis=1)
    idx = lb + jnp.arange(t)[:, None] - dil * jnp.arange(N_KEYS)[None, :]
    valid = idx >= 0
    idx = jnp.maximum(idx, 0)
    kg = kc[:, idx]
    vg = vc[:, idx]
    scores = jnp.einsum('bthe,btjhe->bthj', q, kg,
                        preferred_element_type=jnp.float32) * ATTN_SCALE
    scores = jnp.where(valid[:, None, :], scores, NEG)
    lse = jax.nn.logsumexp(scores, axis=-1)
    p = jnp.exp(scores - lse[..., None])
    o = jnp.einsum('bthj,btjhe->bthe', p, vg.astype(jnp.float32))
    keep = min(window, lb + t)
    new_buf = jnp.stack([kc[:, -keep:], vc[:, -keep:]], axis=2)
    return o, lse, new_buf


def merge_groups(outs, lses, dtype):
    o = jnp.stack(outs, axis=2)
    w = jax.nn.softmax(jnp.stack(lses, axis=2), axis=2)
    o = jnp.sum(w[..., None] * o, axis=2)
    return o.reshape(o.shape[0], o.shape[1], ATTN_OUT_DIM).astype(dtype)


def attend_prompt(q, k, v):
    outs, lses, bufs = [], [], []
    s = k.shape[1]
    for g, (win, dil) in enumerate(GROUPS):
        o, l = dilated_attn_prompt(q[:, :, g], k[:, :, g], v[:, :, g], dil)
        outs.append(o)
        lses.append(l)
        keep = min(win, s)
        bufs.append(jnp.stack([k[:, -keep:, g], v[:, -keep:, g]], axis=2))
    return merge_groups(outs, lses, q.dtype), bufs


def attend_sample(q, k, v, caches):
    outs, lses, bufs = [], [], []
    for g, (win, dil) in enumerate(GROUPS):
        o, l, nb_ = dilated_attn_sample(q[:, :, g], k[:, :, g], v[:, :, g], caches[g], dil, win)
        outs.append(o)
        lses.append(l)
        bufs.append(nb_)
    return merge_groups(outs, lses, q.dtype), bufs


def short_conv(zc, w):
    t = zc.shape[1] - (CONV_WIDTH - 1)
    return w[0] * zc[:, 0:t] + w[1] * zc[:, 1:t + 1] + w[2] * zc[:, 2:t + 2]


def block(x, attend, conv_prev, w):
    (n1, f1i, f1o, nm, w_in, b_gate, conv_w, w_ao, w_co, w_o, n2, f2i, f2o) = w
    x = x + half_swiglu(x, n1, f1i, f1o)
    u = rmsnorm(x, nm)
    q, k, v, c_b, c_c, c_h, g_attn, g_conv = split_proj(u, w_in, b_gate)
    a, kv_bufs = attend(q, k, v)
    zc = jnp.concatenate([conv_prev.astype(x.dtype), c_c * c_h], axis=1)
    c = c_b * short_conv(zc, conv_w)
    new_conv = zc[:, -(CONV_WIDTH - 1):]
    m = g_attn * (a @ w_ao) + g_conv * (c @ w_co)
    x = x + m @ w_o
    x = x + half_swiglu(x, n2, f2i, f2o)
    return x, kv_bufs, new_conv


def setup_inputs(seed: int = 0) -> dict:
    key = jax.random.key(seed)
    ks = jax.random.split(key, 22)
    f32 = jnp.float32
    nrm = lambda kk, shape, scale: jax.random.normal(kk, shape, f32) * scale
    lens = [min(win, PAST_LEN) for win, _ in GROUPS]
    kvshape = lambda n: (DEPTH, DEC_BATCH, n, 2, HEADS_PER_GROUP, HEAD_DIM)
    return {
        'x_prompt': nrm(ks[0], (BATCH, SEQ, D_MODEL), 1.0),
        'x_sample': nrm(ks[1], (DEC_BATCH, DEC_SEQ, D_MODEL), 1.0),
        'cache_kv1': nrm(ks[2], kvshape(lens[0]), 1.0),
        'cache_kv2': nrm(ks[3], kvshape(lens[1]), 1.0),
        'cache_kv3': nrm(ks[4], kvshape(lens[2]), 1.0),
        'state_conv': nrm(ks[5], (DEPTH, DEC_BATCH, CONV_WIDTH - 1, CONV_DIM), 1.0),
        'ffn1_norm': 1.0 + nrm(ks[6], (DEPTH, D_MODEL), 0.01),
        'ffn1_w_in': nrm(ks[7], (DEPTH, D_MODEL, 2 * D_FF), D_MODEL ** -0.5),
        'ffn1_w_out': nrm(ks[8], (DEPTH, D_FF, D_MODEL), D_FF ** -0.5),
        'mix_norm': 1.0 + nrm(ks[9], (DEPTH, D_MODEL), 0.01),
        'w_in': nrm(ks[10], (DEPTH, D_MODEL, IN_DIM), D_MODEL ** -0.5),
        'b_gate': nrm(ks[11], (DEPTH, 2 * D_MODEL), 0.1),
        'conv_w': nrm(ks[12], (DEPTH, CONV_WIDTH, CONV_DIM), CONV_WIDTH ** -0.5),
        'w_attn_out': nrm(ks[13], (DEPTH, ATTN_OUT_DIM, D_MODEL), ATTN_OUT_DIM ** -0.5),
        'w_conv_out': nrm(ks[14], (DEPTH, CONV_DIM, D_MODEL), CONV_DIM ** -0.5),
        'w_out': nrm(ks[15], (DEPTH, D_MODEL, D_MODEL), D_MODEL ** -0.5),
        'ffn2_norm': 1.0 + nrm(ks[16], (DEPTH, D_MODEL), 0.01),
        'ffn2_w_in': nrm(ks[17], (DEPTH, D_MODEL, 2 * D_FF), D_MODEL ** -0.5),
        'ffn2_w_out': nrm(ks[18], (DEPTH, D_FF, D_MODEL), D_FF ** -0.5),
        'final_norm': 1.0 + nrm(ks[19], (D_MODEL,), 0.01),
    }


def reference(x_prompt, x_sample, cache_kv1, cache_kv2, cache_kv3, state_conv,
              ffn1_norm, ffn1_w_in, ffn1_w_out, mix_norm, w_in, b_gate, conv_w,
              w_attn_out, w_conv_out, w_out, ffn2_norm, ffn2_w_in, ffn2_w_out, final_norm):
    yp, ys = x_prompt, x_sample
    kvp = [[], [], []]
    kvs = [[], [], []]
    convp, convs = [], []
    for l in range(DEPTH):
        w = (ffn1_norm[l], ffn1_w_in[l], ffn1_w_out[l], mix_norm[l], w_in[l], b_gate[l],
             conv_w[l], w_attn_out[l], w_conv_out[l], w_out[l],
             ffn2_norm[l], ffn2_w_in[l], ffn2_w_out[l])
        zeros_ctx = jnp.zeros((yp.shape[0], CONV_WIDTH - 1, CONV_DIM), yp.dtype)
        yp, bufs_p, cv_p = block(yp, attend_prompt, zeros_ctx, w)
        att_s = functools.partial(attend_sample, caches=(cache_kv1[l], cache_kv2[l], cache_kv3[l]))
        ys, bufs_s, cv_s = block(ys, att_s, state_conv[l], w)
        for g in range(N_GROUPS):
            kvp[g].append(bufs_p[g])
            kvs[g].append(bufs_s[g])
        convp.append(cv_p)
        convs.append(cv_s)
    yp = rmsnorm(yp, final_norm)
    ys = rmsnorm(ys, final_norm)
    return (yp, ys,
            jnp.stack(kvp[0]), jnp.stack(kvp[1]), jnp.stack(kvp[2]), jnp.stack(convp),
            jnp.stack(kvs[0]), jnp.stack(kvs[1]), jnp.stack(kvs[2]), jnp.stack(convs))
```

```python
import functools

import jax
import jax.numpy as jnp
from jax import lax
from jax.experimental import pallas as pl
from jax.experimental.pallas import tpu as pltpu

F32 = jnp.float32
BF16 = jnp.bfloat16

D_MODEL = 1024
D_FF = 2816
HEAD_DIM = 64
HEADS = 8
GROUP_COLS = HEADS * HEAD_DIM
GROUPS = ((128, 1), (512, 4), (2048, 16))
N_GROUPS = len(GROUPS)
SPAN = 128
ATTN_DIM = N_GROUPS * GROUP_COLS
QKV_DIM = 3 * ATTN_DIM
CONV_DIM = D_MODEL
RMS_EPS = 1e-6
ATTN_SCALE = HEAD_DIM ** -0.5
NEG = -1e30

FF_CHUNK = 256
N_FF_CHUNKS = D_FF // FF_CHUNK
QBLOCK = 128
SAMPLE_ROWS = 8
V7X_VMEM_LIMIT_BYTES = 56 * 1024 * 1024
ROW_TILE = 512


def _params(n_axes):
    return pltpu.CompilerParams(
        dimension_semantics=("arbitrary",) * n_axes,
        vmem_limit_bytes=V7X_VMEM_LIMIT_BYTES)


def _resident(block_shape, index_map):
    return pl.BlockSpec(block_shape, index_map, pipeline_mode=pl.Buffered(1))


def _rms_inv(x):
    return lax.rsqrt(jnp.mean(x * x, axis=-1, keepdims=True) + RMS_EPS)


def _ffn_kernel(x_ref, g_ref, wi_ref, wo_ref, fg_ref, o_ref, h_s, acc_s, *, final):
    x = x_ref[...]
    h_s[...] = ((x * _rms_inv(x)) * g_ref[...]).astype(BF16)
    for c in range(N_FF_CHUNKS):
        gu = jnp.dot(h_s[...], wi_ref[c], preferred_element_type=F32)
        gate = gu[:, :FF_CHUNK]
        up = gu[:, FF_CHUNK:]
        act = ((gate * jax.nn.sigmoid(gate)) * up).astype(BF16)
        part = jnp.dot(act, wo_ref[c], preferred_element_type=F32)
        if c == 0:
            acc_s[...] = part
        else:
            acc_s[...] += part
    y = x_ref[...] + 0.5 * acc_s[...]
    if final:
        y = (y * _rms_inv(y)) * fg_ref[...]
    o_ref[...] = y


def _ffn_call(x, gains, wi, wo, final_gain, layer, tm, final, name):
    n = x.shape[0]
    return pl.pallas_call(
        functools.partial(_ffn_kernel, final=final),
        grid=(n // tm,),
        in_specs=[
            pl.BlockSpec((tm, D_MODEL), lambda i: (i, 0)),
            pl.BlockSpec((None, 1, D_MODEL), lambda i: (layer, 0, 0)),
            _resident((None, N_FF_CHUNKS, D_MODEL, 2 * FF_CHUNK), lambda i: (layer, 0, 0, 0)),
            _resident((None, N_FF_CHUNKS, FF_CHUNK, D_MODEL), lambda i: (layer, 0, 0, 0)),
            pl.BlockSpec((1, D_MODEL), lambda i: (0, 0)),
        ],
        out_specs=pl.BlockSpec((tm, D_MODEL), lambda i: (i, 0)),
        out_shape=jax.ShapeDtypeStruct((n, D_MODEL), F32),
        scratch_shapes=[pltpu.VMEM((tm, D_MODEL), BF16), pltpu.VMEM((tm, D_MODEL), F32)],
        compiler_params=_params(1),
        name=name,
    )(x, gains, wi, wo, final_gain)


def _qkv_kernel(x_ref, g_ref, w_ref, qkv_ref, tail_ref, u_s):
    x = x_ref[...]
    u_s[...] = ((x * _rms_inv(x)) * g_ref[...]).astype(BF16)
    for c in range(QKV_DIM // GROUP_COLS):
        sl = slice(c * GROUP_COLS, (c + 1) * GROUP_COLS)
        z = jnp.dot(u_s[...], w_ref[:, sl], preferred_element_type=F32)
        if c < N_GROUPS:
            z = z * ATTN_SCALE
        qkv_ref[:, sl] = z.astype(BF16)
        tail_ref[:, sl] = z


def _qkv_call(x, gains, w, layer, tm, seq_rows, tail_rows, name):
    n = x.shape[0]
    tiles_per_seq = seq_rows // tm
    tail_tiles = tail_rows // tm
    n_seq = n // seq_rows

    def tail_map(i):
        j = jnp.maximum(i % tiles_per_seq - (tiles_per_seq - tail_tiles), 0)
        return ((i // tiles_per_seq) * tail_tiles + j, 0)

    return pl.pallas_call(
        _qkv_kernel,
        grid=(n // tm,),
        in_specs=[
            pl.BlockSpec((tm, D_MODEL), lambda i: (i, 0)),
            pl.BlockSpec((None, 1, D_MODEL), lambda i: (layer, 0, 0)),
            _resident((None, D_MODEL, QKV_DIM), lambda i: (layer, 0, 0)),
        ],
        out_specs=[
            pl.BlockSpec((tm, QKV_DIM), lambda i: (i, 0)),
            pl.BlockSpec((tm, QKV_DIM), tail_map),
        ],
        out_shape=[
            jax.ShapeDtypeStruct((n, QKV_DIM), BF16),
            jax.ShapeDtypeStruct((n_seq * tail_rows, QKV_DIM), F32),
        ],
        scratch_shapes=[pltpu.VMEM((tm, D_MODEL), BF16)],
        compiler_params=_params(1),
        name=name,
    )(x, gains, w)


CONV_CHUNK = 256


def _conv_gate_kernel(*refs, sample, tiles_per_seq):
    if sample:
        (x_ref, g_ref, wc_ref, wg_ref, bg_ref, cw_ref, wco_ref, st_ref,
         ga_ref, mc_ref, p_ref, u_s, c_s) = refs
    else:
        (x_ref, g_ref, wc_ref, wg_ref, bg_ref, cw_ref, wco_ref,
         ga_ref, mc_ref, p_ref, u_s, c_s, halo_s) = refs
    tm = x_ref.shape[0]
    x = x_ref[...]
    u_s[...] = ((x * _rms_inv(x)) * g_ref[...]).astype(BF16)
    row = lax.broadcasted_iota(jnp.int32, (tm, CONV_CHUNK), 0)

    if not sample:
        @pl.when(pl.program_id(0) % tiles_per_seq == 0)
        def _():
            halo_s[...] = jnp.zeros_like(halo_s)

    for c in range(CONV_DIM // CONV_CHUNK):
        sl = slice(c * CONV_CHUNK, (c + 1) * CONV_CHUNK)
        u = u_s[...]
        cb = jnp.dot(u, wc_ref[:, sl], preferred_element_type=F32)
        cc = jnp.dot(u, wc_ref[:, CONV_DIM + c * CONV_CHUNK:CONV_DIM + (c + 1) * CONV_CHUNK],
                     preferred_element_type=F32)
        ch = jnp.dot(u, wc_ref[:, 2 * CONV_DIM + c * CONV_CHUNK:2 * CONV_DIM + (c + 1) * CONV_CHUNK],
                     preferred_element_type=F32)
        p = cc * ch
        if sample:
            p_ref[:, sl] = p
            p = jnp.where((row & (SAMPLE_ROWS - 1)) >= SAMPLE_ROWS - 2, st_ref[:, sl], p)
            r1 = pltpu.roll(p, 1, axis=0)
            r2 = pltpu.roll(p, 2, axis=0)
        else:
            r1 = pltpu.roll(p, 1, axis=0)
            r2 = pltpu.roll(p, 2, axis=0)
            h = halo_s[:, sl]
            r1 = jnp.where(row == 0, h[7:8], r1)
            r2 = jnp.where(row == 0, h[6:7], jnp.where(row == 1, h[7:8], r2))
            halo_s[:, sl] = p[tm - 8:tm]
            p_ref[:, sl] = p[tm - 8:tm]
        conv = cw_ref[0:1, sl] * r2 + cw_ref[1:2, sl] * r1 + cw_ref[2:3, sl] * p
        c_s[:, sl] = (cb * conv).astype(BF16)

    cproj = jnp.dot(c_s[...], wco_ref[...], preferred_element_type=F32)
    za = jnp.dot(u_s[...], wg_ref[:, :D_MODEL], preferred_element_type=F32) + bg_ref[:, :D_MODEL]
    ga_ref[...] = jax.nn.sigmoid(za)
    zc = jnp.dot(u_s[...], wg_ref[:, D_MODEL:], preferred_element_type=F32) + bg_ref[:, D_MODEL:]
    mc_ref[...] = jax.nn.sigmoid(zc) * cproj


def _conv_gate_call(x, gains, wc, wg, bg, cw, wco, state, layer, tm, seq_rows, name):
    n = x.shape[0]
    sample = state is not None
    tiles_per_seq = seq_rows // tm
    row_spec = pl.BlockSpec((tm, D_MODEL), lambda i: (i, 0))
    in_specs = [
        row_spec,
        pl.BlockSpec((None, 1, D_MODEL), lambda i: (layer, 0, 0)),
        _resident((None, D_MODEL, 3 * CONV_DIM), lambda i: (layer, 0, 0)),
        _resident((None, D_MODEL, 2 * D_MODEL), lambda i: (layer, 0, 0)),
        pl.BlockSpec((None, 1, 2 * D_MODEL), lambda i: (layer, 0, 0)),
        pl.BlockSpec((None, 3, CONV_DIM), lambda i: (layer, 0, 0)),
        _resident((None, CONV_DIM, D_MODEL), lambda i: (layer, 0, 0)),
    ]
    args = [x, gains, wc, wg, bg, cw, wco]
    scratch = [pltpu.VMEM((tm, D_MODEL), BF16), pltpu.VMEM((tm, CONV_DIM), BF16)]
    if sample:
        in_specs.append(row_spec)
        args.append(state)
        p_spec = row_spec
        p_shape = jax.ShapeDtypeStruct((n, CONV_DIM), F32)
    else:
        p_spec = pl.BlockSpec((8, CONV_DIM), lambda i: (i // tiles_per_seq, 0))
        p_shape = jax.ShapeDtypeStruct((n // seq_rows * 8, CONV_DIM), F32)
        scratch.append(pltpu.VMEM((8, CONV_DIM), F32))
    return pl.pallas_call(
        functools.partial(_conv_gate_kernel, sample=sample, tiles_per_seq=tiles_per_seq),
        grid=(n // tm,),
        in_specs=in_specs,
        out_specs=[row_spec, row_spec, p_spec],
        out_shape=[jax.ShapeDtypeStruct((n, D_MODEL), F32),
                   jax.ShapeDtypeStruct((n, D_MODEL), F32), p_shape],
        scratch_shapes=scratch,
        compiler_params=_params(1),
        name=name,
    )(*args)


def _attn_prompt_kernel(q_ref, kp_ref, kc_ref, vp_ref, vc_ref, o_ref, l_ref):
    n = pl.program_id(2)
    k = jnp.concatenate([kp_ref[...], kc_ref[...]], axis=0)
    v = jnp.concatenate([vp_ref[...], vc_ref[...]], axis=0)
    row = lax.broadcasted_iota(jnp.int32, (2 * QBLOCK, 2 * QBLOCK), 0)
    col = lax.broadcasted_iota(jnp.int32, (2 * QBLOCK, 2 * QBLOCK), 1)
    qi = row & (QBLOCK - 1)
    prev_lo = qi + jnp.where(n > 0, 0, QBLOCK)
    valid = ((col >= QBLOCK) & (col - QBLOCK <= qi)) | ((col < QBLOCK) & (col >= prev_lo))
    even = lax.broadcasted_iota(jnp.int32, (QBLOCK, 2 * HEAD_DIM), 1) < HEAD_DIM
    for j in range(HEADS // 2):
        sl = slice(2 * HEAD_DIM * j, 2 * HEAD_DIM * (j + 1))
        qp = q_ref[:, sl].astype(F32)
        lhs = jnp.concatenate([jnp.where(even, qp, 0.0), jnp.where(even, 0.0, qp)],
                              axis=0).astype(BF16)
        s = lax.dot_general(lhs, k[:, sl], (((1,), (1,)), ((), ())),
                            preferred_element_type=F32)
        s = jnp.where(valid, s, NEG)
        m = jnp.max(s, axis=1, keepdims=True)
        p = jnp.exp(s - m)
        l = jnp.sum(p, axis=1, keepdims=True)
        pv = jnp.dot(p.astype(BF16), v[:, sl], preferred_element_type=F32)
        o = pv / l
        lse = jnp.broadcast_to(m + jnp.log(l), o.shape)
        o_ref[:, sl] = jnp.where(even, o[:QBLOCK], o[QBLOCK:])
        l_ref[:, sl] = jnp.where(even, lse[:QBLOCK], lse[QBLOCK:])


def _attn_prompt_call(qkv, batch, seq, g, dil, name):
    length = seq // dil
    nb = length // QBLOCK
    cols = QKV_DIM // GROUP_COLS
    view = qkv.reshape(batch, length, dil * QKV_DIM)

    def spec(off, prev):
        if prev:
            return pl.BlockSpec((None, QBLOCK, GROUP_COLS),
                                lambda b, r, n: (b, jnp.maximum(n - 1, 0), r * cols + off + g))
        return pl.BlockSpec((None, QBLOCK, GROUP_COLS),
                            lambda b, r, n: (b, n, r * cols + off + g))

    out_spec = pl.BlockSpec((None, QBLOCK, GROUP_COLS), lambda b, r, n: (b, n, r))
    out_shape = jax.ShapeDtypeStruct((batch, length, dil * GROUP_COLS), F32)
    o, lse = pl.pallas_call(
        _attn_prompt_kernel,
        grid=(batch, dil, nb),
        in_specs=[spec(0, False), spec(N_GROUPS, True), spec(N_GROUPS, False),
                  spec(2 * N_GROUPS, True), spec(2 * N_GROUPS, False)],
        out_specs=[out_spec, out_spec],
        out_shape=[out_shape, out_shape],
        compiler_params=_params(3),
        name=name,
    )(view, view, view, view, view)
    return o.reshape(batch * seq, GROUP_COLS), lse.reshape(batch * seq, GROUP_COLS)


def _attn_sample_kernel(*refs, lb, dil, dec, aliased):
    if aliased:
        q_ref, kn_ref, vn_ref, c_ref, _, o_ref, l_ref, nc_ref = refs
    else:
        q_ref, kn_ref, vn_ref, c_ref, o_ref, l_ref, nc_ref = refs
    q = q_ref[...]
    sub = lax.broadcasted_iota(jnp.int32, (SAMPLE_ROWS, GROUP_COLS), 0)
    lane = lax.broadcasted_iota(jnp.int32, (SAMPLE_ROWS, GROUP_COLS), 1)
    diag = sub == lane // HEAD_DIM
    qb = jnp.concatenate(
        [jnp.where(diag, jnp.broadcast_to(q[t:t + 1], diag.shape), 0.0) for t in range(dec)],
        axis=0).astype(BF16)
    rows = dec * HEADS
    kt = c_ref[0].astype(BF16)
    vt = c_ref[1].astype(BF16)
    s = jnp.dot(qb, kt, preferred_element_type=F32)
    t_of = lax.broadcasted_iota(jnp.int32, (rows, lb), 0) // HEADS
    diff = lax.broadcasted_iota(jnp.int32, (rows, lb), 1) - t_of
    valid = (diff >= 0) & ((diff & (dil - 1)) == 0)
    s = jnp.where(valid, s, NEG)
    m = jnp.max(s, axis=1, keepdims=True)

    qbf = qb.astype(F32)
    knr = kn_ref[...].astype(BF16).astype(F32)
    vnr = vn_ref[...].astype(BF16).astype(F32)
    t_col = lax.broadcasted_iota(jnp.int32, (rows, 1), 0) // HEADS
    s_new = []
    for t2 in range(dec):
        s2 = jnp.sum(qbf * knr[t2:t2 + 1], axis=1, keepdims=True)
        d2 = t_col - t2
        s2 = jnp.where((d2 >= 0) & ((d2 & (dil - 1)) == 0), s2, NEG)
        s_new.append(s2)
        m = jnp.maximum(m, s2)
    p = jnp.exp(s - m)
    l = jnp.sum(p, axis=1, keepdims=True)
    acc = lax.dot_general(p.astype(BF16), vt, (((1,), (1,)), ((), ())),
                          preferred_element_type=F32)
    for t2 in range(dec):
        pn = jnp.exp(s_new[t2] - m)
        l = l + pn
        acc = acc + pn * vnr[t2:t2 + 1]
    o = acc / l
    lse = jnp.broadcast_to(m + jnp.log(l), o.shape)
    o_rows, l_rows = [], []
    for t in range(dec):
        blk = slice(t * HEADS, (t + 1) * HEADS)
        o_rows.append(jnp.sum(jnp.where(diag, o[blk], 0.0), axis=0, keepdims=True))
        l_rows.append(jnp.sum(jnp.where(diag, lse[blk], 0.0), axis=0, keepdims=True))
    pad = jnp.zeros((SAMPLE_ROWS - dec, GROUP_COLS), F32)
    o_ref[...] = jnp.concatenate(o_rows + [pad], axis=0)
    l_ref[...] = jnp.concatenate(l_rows + [pad], axis=0)

    lane_t = lax.broadcasted_iota(jnp.int32, (GROUP_COLS, 128), 1)
    for kv, new_ref in ((0, kn_ref), (1, vn_ref)):
        rolled = pltpu.roll(c_ref[kv], lb - dec, axis=1)
        nc_ref[kv] = rolled
        shifted = jnp.where(sub >= SAMPLE_ROWS - dec, pltpu.roll(new_ref[...], SAMPLE_ROWS - dec, axis=0), 0.0)
        padded = jnp.concatenate([jnp.zeros((128 - SAMPLE_ROWS, GROUP_COLS), F32), shifted], axis=0)
        new_t = padded.T
        nc_ref[kv, :, lb - 128:] = jnp.where(lane_t >= 128 - dec, new_t, rolled[:, lb - 128:])


def _attn_sample_call(qkv32, cache, prev_out, layer, g, dil, dec, name):
    depth, n_seq, _, _, lb = cache.shape
    n = qkv32.shape[0]
    row = lambda off: pl.BlockSpec((SAMPLE_ROWS, GROUP_COLS), lambda s: (s, off + g))
    cache_spec = pl.BlockSpec((None, None, 2, GROUP_COLS, lb), lambda s: (layer, s, 0, 0, 0))
    in_specs = [row(0), row(N_GROUPS), row(2 * N_GROUPS), cache_spec]
    args = [qkv32, qkv32, qkv32, cache]
    aliases = {}
    if prev_out is not None:
        in_specs.append(pl.BlockSpec(memory_space=pl.ANY))
        args.append(prev_out)
        aliases = {4: 2}
    out_row = pl.BlockSpec((SAMPLE_ROWS, GROUP_COLS), lambda s: (s, 0))
    return pl.pallas_call(
        functools.partial(_attn_sample_kernel, lb=lb, dil=dil, dec=dec, aliased=prev_out is not None),
        grid=(n_seq,),
        in_specs=in_specs,
        out_specs=[out_row, out_row, cache_spec],
        out_shape=[jax.ShapeDtypeStruct((n, GROUP_COLS), F32),
                   jax.ShapeDtypeStruct((n, GROUP_COLS), F32),
                   jax.ShapeDtypeStruct(cache.shape, F32)],
        input_output_aliases=aliases,
        compiler_params=_params(1),
        name=name,
    )(*args)


def _mix_out_kernel(x_ref, o1_ref, l1_ref, o2_ref, l2_ref, o3_ref, l3_ref, ga_ref, mc_ref,
                    wao_ref, wo_ref, out_ref):
    l1, l2, l3 = l1_ref[...], l2_ref[...], l3_ref[...]
    mx = jnp.maximum(jnp.maximum(l1, l2), l3)
    e1, e2, e3 = jnp.exp(l1 - mx), jnp.exp(l2 - mx), jnp.exp(l3 - mx)
    a = (e1 * o1_ref[...] + e2 * o2_ref[...] + e3 * o3_ref[...]) / (e1 + e2 + e3)
    t = jnp.dot(a.astype(BF16), wao_ref[...], preferred_element_type=F32)
    m = ga_ref[...] * t + mc_ref[...]
    out_ref[...] = x_ref[...] + jnp.dot(m.astype(BF16), wo_ref[...], preferred_element_type=F32)


def _mix_out_call(x, attn, ga, mc, wao, wo, layer, tm, name):
    n = x.shape[0]
    wide = pl.BlockSpec((tm, D_MODEL), lambda i: (i, 0))
    half = pl.BlockSpec((tm, GROUP_COLS), lambda i: (i, 0))
    flat = [a for pair in attn for a in pair]
    return pl.pallas_call(
        _mix_out_kernel,
        grid=(n // tm,),
        in_specs=[wide] + [half] * 6 + [wide, wide,
                  _resident((None, GROUP_COLS, D_MODEL), lambda i: (layer, 0, 0)),
                  _resident((None, D_MODEL, D_MODEL), lambda i: (layer, 0, 0))],
        out_specs=wide,
        out_shape=jax.ShapeDtypeStruct((n, D_MODEL), F32),
        compiler_params=_params(1),
        name=name,
    )(x, *flat, ga, mc, wao, wo)


def _ffn_weights(w_in, w_out):
    depth = w_in.shape[0]
    gate = w_in[:, :, :D_FF].reshape(depth, D_MODEL, N_FF_CHUNKS, FF_CHUNK)
    up = w_in[:, :, D_FF:].reshape(depth, D_MODEL, N_FF_CHUNKS, FF_CHUNK)
    wi = jnp.concatenate([gate, up], axis=-1).transpose(0, 2, 1, 3).astype(BF16)
    wo = w_out.reshape(depth, N_FF_CHUNKS, FF_CHUNK, D_MODEL).astype(BF16)
    return wi, wo


def kernel(x_prompt, x_sample, cache_kv1, cache_kv2, cache_kv3, state_conv, ffn1_norm, ffn1_w_in,
           ffn1_w_out, mix_norm, w_in, b_gate, conv_w, w_attn_out, w_conv_out, w_out, ffn2_norm,
           ffn2_w_in, ffn2_w_out, final_norm):
    batch, seq, _ = x_prompt.shape
    n_seq, dec, _ = x_sample.shape
    depth = w_in.shape[0]
    caches = (cache_kv1, cache_kv2, cache_kv3)
    assert dec <= SAMPLE_ROWS - 2 and seq % (GROUPS[-1][1] * QBLOCK) == 0
    for cache, (win, dil) in zip(caches, GROUPS):
        assert cache.shape[2] == win and win == SPAN * dil and (dil == 1 or dec <= dil)

    f1_wi, f1_wo = _ffn_weights(ffn1_w_in, ffn1_w_out)
    f2_wi, f2_wo = _ffn_weights(ffn2_w_in, ffn2_w_out)
    w_qkv = w_in[:, :, :QKV_DIM].astype(BF16)
    w_conv = w_in[:, :, QKV_DIM:QKV_DIM + 3 * CONV_DIM].astype(BF16)
    w_gate = w_in[:, :, QKV_DIM + 3 * CONV_DIM:].astype(BF16)
    w_ao = w_attn_out.astype(BF16)
    w_co = w_conv_out.astype(BF16)
    w_o = w_out.astype(BF16)
    g1 = ffn1_norm.reshape(depth, 1, D_MODEL)
    gm = mix_norm.reshape(depth, 1, D_MODEL)
    g2 = ffn2_norm.reshape(depth, 1, D_MODEL)
    gf = final_norm.reshape(1, D_MODEL)
    bg = b_gate.reshape(depth, 1, 2 * D_MODEL)

    caches_t = [c.transpose(0, 1, 3, 4, 5, 2).reshape(depth, n_seq, 2, GROUP_COLS, c.shape[2])
                for c in caches]

    xp = x_prompt.reshape(batch * seq, D_MODEL)
    xs = jnp.pad(x_sample, ((0, 0), (0, SAMPLE_ROWS - dec), (0, 0))).reshape(n_seq * SAMPLE_ROWS, D_MODEL)
    ns = xs.shape[0]
    tail_rows = GROUPS[-1][0]

    kv_prompt = [[] for _ in GROUPS]
    conv_prompt, conv_sample = [], []
    new_caches = [None] * N_GROUPS
    for l in range(depth):
        last = l == depth - 1
        xs = _ffn_call(xs, g1, f1_wi, f1_wo, gf, l, ns, False, f"s_ffn1_{l}")
        _, s_tail = _qkv_call(xs, gm, w_qkv, l, ns, ns, ns, f"s_qkv_{l}")
        st = jnp.pad(state_conv[l], ((0, 0), (SAMPLE_ROWS - 2, 0), (0, 0)))
        st = jnp.roll(st, -1, axis=0).reshape(ns, CONV_DIM)
        s_ga, s_mc, s_p = _conv_gate_call(xs, gm, w_conv, w_gate, bg, conv_w, w_co, st, l, ns, ns,
                                          f"s_conv_{l}")
        conv_sample.append(s_p.reshape(n_seq, SAMPLE_ROWS, CONV_DIM)[:, dec - 2:dec])
        s_attn = []
        for g, (win, dil) in enumerate(GROUPS):
            o, lse, new_caches[g] = _attn_sample_call(s_tail, caches_t[g], new_caches[g], l, g, dil,
                                                      dec, f"s_attn{g}_{l}")
            s_attn.append((o, lse))
        xs = _mix_out_call(xs, s_attn, s_ga, s_mc, w_ao, w_o, l, ns, f"s_mix_{l}")
        xs = _ffn_call(xs, g2, f2_wi, f2_wo, gf, l, ns, last, f"s_ffn2_{l}")

        xp = _ffn_call(xp, g1, f1_wi, f1_wo, gf, l, ROW_TILE, False, f"p_ffn1_{l}")
        qkv, p_tail = _qkv_call(xp, gm, w_qkv, l, ROW_TILE, seq, tail_rows, f"p_qkv_{l}")
        p_ga, p_mc, p_p = _conv_gate_call(xp, gm, w_conv, w_gate, bg, conv_w, w_co, None, l,
                                          ROW_TILE, seq, f"p_conv_{l}")
        conv_prompt.append(p_p.reshape(batch, 8, CONV_DIM)[:, 6:8])
        p_tail = p_tail.reshape(batch, tail_rows, QKV_DIM)
        p_attn = []
        for g, (win, dil) in enumerate(GROUPS):
            p_attn.append(_attn_prompt_call(qkv, batch, seq, g, dil, f"p_attn{g}_{l}"))
            keep = min(win, seq)
            kcol = ATTN_DIM + g * GROUP_COLS
            vcol = 2 * ATTN_DIM + g * GROUP_COLS
            k_t = p_tail[:, tail_rows - keep:, kcol:kcol + GROUP_COLS]
            v_t = p_tail[:, tail_rows - keep:, vcol:vcol + GROUP_COLS]
            kv_prompt[g].append(jnp.stack([k_t, v_t], axis=2).reshape(batch, keep, 2, HEADS, HEAD_DIM))
        xp = _mix_out_call(xp, p_attn, p_ga, p_mc, w_ao, w_o, l, ROW_TILE, f"p_mix_{l}")
        xp = _ffn_call(xp, g2, f2_wi, f2_wo, gf, l, ROW_TILE, last, f"p_ffn2_{l}")

    y_prompt = xp.reshape(batch, seq, D_MODEL)
    y_sample = xs.reshape(n_seq, SAMPLE_ROWS, D_MODEL)[:, :dec]
    kv_sample = [c.reshape(depth, n_seq, 2, HEADS, HEAD_DIM, c.shape[-1]).transpose(0, 1, 5, 2, 3, 4)
                 for c in new_caches]
    return (y_prompt, y_sample,
            jnp.stack(kv_prompt[0]), jnp.stack(kv_prompt[1]), jnp.stack(kv_prompt[2]),
            jnp.stack(conv_prompt),
            kv_sample[0], kv_sample[1], kv_sample[2],
            jnp.stack(conv_sample))
```

```python
import functools

import jax
import jax.numpy as jnp
from jax import lax
from jax.experimental import pallas as pl
from jax.experimental.pallas import tpu as pltpu

F32 = jnp.float32
BF16 = jnp.bfloat16

D_MODEL = 1024
D_FF = 2816
HEAD_DIM = 64
HEADS = 8
GROUP_COLS = HEADS * HEAD_DIM
GROUPS = ((128, 1), (512, 4), (2048, 16))
N_GROUPS = len(GROUPS)
SPAN = 128
ATTN_DIM = N_GROUPS * GROUP_COLS
QKV_DIM = 3 * ATTN_DIM
CONV_DIM = D_MODEL
RMS_EPS = 1e-6
ATTN_SCALE = HEAD_DIM ** -0.5
NEG = -1e30

FF_CHUNK = 256
N_FF_CHUNKS = D_FF // FF_CHUNK
QBLOCK = 128
LANES = 128
LANE_TILES = GROUP_COLS // LANES
SAMPLE_ROWS = 8
V7X_VMEM_LIMIT_BYTES = 56 * 1024 * 1024
ROW_TILE = 512


def _params(n_axes):
    return pltpu.CompilerParams(
        dimension_semantics=("arbitrary",) * n_axes,
        vmem_limit_bytes=V7X_VMEM_LIMIT_BYTES)


def _resident(block_shape, index_map):
    return pl.BlockSpec(block_shape, index_map, pipeline_mode=pl.Buffered(1))


def _rms_inv(x):
    return lax.rsqrt(jnp.mean(x * x, axis=-1, keepdims=True) + RMS_EPS)


def _ffn_kernel(x_ref, g_ref, wi_ref, wo_ref, fg_ref, o_ref, h_s, acc_s, *, final):
    x = x_ref[...]
    h_s[...] = ((x * _rms_inv(x)) * g_ref[...]).astype(BF16)
    for c in range(N_FF_CHUNKS):
        lo = c * FF_CHUNK
        gate = jnp.dot(h_s[...], wi_ref[:, lo:lo + FF_CHUNK], preferred_element_type=F32)
        up = jnp.dot(h_s[...], wi_ref[:, D_FF + lo:D_FF + lo + FF_CHUNK],
                     preferred_element_type=F32)
        act = ((gate * jax.nn.sigmoid(gate)) * up).astype(BF16)
        part = jnp.dot(act, wo_ref[lo:lo + FF_CHUNK, :], preferred_element_type=F32)
        if c == 0:
            acc_s[...] = part
        else:
            acc_s[...] += part
    y = x_ref[...] + 0.5 * acc_s[...]
    if final:
        y = (y * _rms_inv(y)) * fg_ref[...]
    o_ref[...] = y


def _ffn_call(x, gains, wi, wo, final_gain, layer, tm, final, name):
    n = x.shape[0]
    return pl.pallas_call(
        functools.partial(_ffn_kernel, final=final),
        grid=(n // tm,),
        in_specs=[
            pl.BlockSpec((tm, D_MODEL), lambda i: (i, 0)),
            pl.BlockSpec((None, 1, D_MODEL), lambda i: (layer, 0, 0)),
            _resident((None, D_MODEL, 2 * D_FF), lambda i: (layer, 0, 0)),
            _resident((None, D_FF, D_MODEL), lambda i: (layer, 0, 0)),
            pl.BlockSpec((1, D_MODEL), lambda i: (0, 0)),
        ],
        out_specs=pl.BlockSpec((tm, D_MODEL), lambda i: (i, 0)),
        out_shape=jax.ShapeDtypeStruct((n, D_MODEL), F32),
        scratch_shapes=[pltpu.VMEM((tm, D_MODEL), BF16), pltpu.VMEM((tm, D_MODEL), F32)],
        compiler_params=_params(1),
        name=name,
    )(x, gains, wi, wo, final_gain)


def _qkv_kernel(x_ref, g_ref, w_ref, *refs):
    group_refs, tail_ref, u_s, z_s = refs[:N_GROUPS], refs[N_GROUPS], refs[N_GROUPS + 1], refs[N_GROUPS + 2]
    tm = x_ref.shape[0]
    x = x_ref[...]
    u_s[...] = ((x * _rms_inv(x)) * g_ref[...]).astype(BF16)
    for c in range(QKV_DIM // GROUP_COLS):
        kind, g = divmod(c, N_GROUPS)
        sl = slice(c * GROUP_COLS, (c + 1) * GROUP_COLS)
        z = jnp.dot(u_s[...], w_ref[:, sl], preferred_element_type=F32)
        if kind == 0:
            z = z * ATTN_SCALE
        tail_ref[:, sl] = z
        out = group_refs[g]
        dil = out.shape[0]
        osl = slice(kind * GROUP_COLS, (kind + 1) * GROUP_COLS)
        if dil == 1:
            out[0, :, osl] = z.astype(BF16)
        else:
            for t in range(LANE_TILES):
                z_s[t] = z[:, t * LANES:(t + 1) * LANES]
            for r in range(dil):
                for t in range(LANE_TILES):
                    lo = kind * GROUP_COLS + t * LANES
                    out[r, :, lo:lo + LANES] = z_s[t, pl.ds(r, tm // dil, stride=dil), :].astype(BF16)


def _qkv_call(x, gains, w, layer, tm, seq_rows, tail_rows, name):
    n = x.shape[0]
    tiles_per_seq = seq_rows // tm
    tail_tiles = tail_rows // tm
    n_seq = n // seq_rows

    def tail_map(i):
        j = jnp.maximum(i % tiles_per_seq - (tiles_per_seq - tail_tiles), 0)
        return ((i // tiles_per_seq) * tail_tiles + j, 0)

    group_specs = [pl.BlockSpec((None, dil, tm // dil, 3 * GROUP_COLS),
                                lambda i: (i // tiles_per_seq, 0, i % tiles_per_seq, 0))
                   for _, dil in GROUPS]
    group_shapes = [jax.ShapeDtypeStruct((n_seq, dil, seq_rows // dil, 3 * GROUP_COLS), BF16)
                    for _, dil in GROUPS]
    return pl.pallas_call(
        _qkv_kernel,
        grid=(n // tm,),
        in_specs=[
            pl.BlockSpec((tm, D_MODEL), lambda i: (i, 0)),
            pl.BlockSpec((None, 1, D_MODEL), lambda i: (layer, 0, 0)),
            _resident((None, D_MODEL, QKV_DIM), lambda i: (layer, 0, 0)),
        ],
        out_specs=group_specs + [pl.BlockSpec((tm, QKV_DIM), tail_map)],
        out_shape=group_shapes + [jax.ShapeDtypeStruct((n_seq * tail_rows, QKV_DIM), F32)],
        scratch_shapes=[pltpu.VMEM((tm, D_MODEL), BF16), pltpu.VMEM((LANE_TILES, tm, LANES), F32)],
        compiler_params=_params(1),
        name=name,
    )(x, gains, w)


CONV_CHUNK = 256


def _conv_gate_kernel(*refs, sample, tiles_per_seq):
    if sample:
        (x_ref, g_ref, wc_ref, wg_ref, bg_ref, cw_ref, wco_ref, st_ref,
         ga_ref, mc_ref, p_ref, u_s, c_s) = refs
    else:
        (x_ref, g_ref, wc_ref, wg_ref, bg_ref, cw_ref, wco_ref,
         ga_ref, mc_ref, p_ref, u_s, c_s, halo_s) = refs
    tm = x_ref.shape[0]
    x = x_ref[...]
    u_s[...] = ((x * _rms_inv(x)) * g_ref[...]).astype(BF16)
    row = lax.broadcasted_iota(jnp.int32, (tm, CONV_CHUNK), 0)

    if not sample:
        @pl.when(pl.program_id(0) % tiles_per_seq == 0)
        def _():
            halo_s[...] = jnp.zeros_like(halo_s)

    for c in range(CONV_DIM // CONV_CHUNK):
        sl = slice(c * CONV_CHUNK, (c + 1) * CONV_CHUNK)
        u = u_s[...]
        cb = jnp.dot(u, wc_ref[:, sl], preferred_element_type=F32)
        cc = jnp.dot(u, wc_ref[:, CONV_DIM + c * CONV_CHUNK:CONV_DIM + (c + 1) * CONV_CHUNK],
                     preferred_element_type=F32)
        ch = jnp.dot(u, wc_ref[:, 2 * CONV_DIM + c * CONV_CHUNK:2 * CONV_DIM + (c + 1) * CONV_CHUNK],
                     preferred_element_type=F32)
        p = cc * ch
        if sample:
            p_ref[:, sl] = p
            p = jnp.where((row & (SAMPLE_ROWS - 1)) >= SAMPLE_ROWS - 2, st_ref[:, sl], p)
            r1 = pltpu.roll(p, 1, axis=0)
            r2 = pltpu.roll(p, 2, axis=0)
        else:
            r1 = pltpu.roll(p, 1, axis=0)
            r2 = pltpu.roll(p, 2, axis=0)
            h = halo_s[:, sl]
            r1 = jnp.where(row == 0, h[7:8], r1)
            r2 = jnp.where(row == 0, h[6:7], jnp.where(row == 1, h[7:8], r2))
            halo_s[:, sl] = p[tm - 8:tm]
            p_ref[:, sl] = p[tm - 8:tm]
        conv = cw_ref[0:1, sl] * r2 + cw_ref[1:2, sl] * r1 + cw_ref[2:3, sl] * p
        c_s[:, sl] = (cb * conv).astype(BF16)

    cproj = jnp.dot(c_s[...], wco_ref[...], preferred_element_type=F32)
    za = jnp.dot(u_s[...], wg_ref[:, :D_MODEL], preferred_element_type=F32) + bg_ref[:, :D_MODEL]
    ga_ref[...] = jax.nn.sigmoid(za)
    zc = jnp.dot(u_s[...], wg_ref[:, D_MODEL:], preferred_element_type=F32) + bg_ref[:, D_MODEL:]
    mc_ref[...] = jax.nn.sigmoid(zc) * cproj


def _conv_gate_call(x, gains, wc, wg, bg, cw, wco, state, layer, tm, seq_rows, name):
    n = x.shape[0]
    sample = state is not None
    tiles_per_seq = seq_rows // tm
    row_spec = pl.BlockSpec((tm, D_MODEL), lambda i: (i, 0))
    in_specs = [
        row_spec,
        pl.BlockSpec((None, 1, D_MODEL), lambda i: (layer, 0, 0)),
        _resident((None, D_MODEL, 3 * CONV_DIM), lambda i: (layer, 0, 0)),
        _resident((None, D_MODEL, 2 * D_MODEL), lambda i: (layer, 0, 0)),
        pl.BlockSpec((None, 1, 2 * D_MODEL), lambda i: (layer, 0, 0)),
        pl.BlockSpec((None, 3, CONV_DIM), lambda i: (layer, 0, 0)),
        _resident((None, CONV_DIM, D_MODEL), lambda i: (layer, 0, 0)),
    ]
    args = [x, gains, wc, wg, bg, cw, wco]
    scratch = [pltpu.VMEM((tm, D_MODEL), BF16), pltpu.VMEM((tm, CONV_DIM), BF16)]
    if sample:
        in_specs.append(row_spec)
        args.append(state)
        p_spec = row_spec
        p_shape = jax.ShapeDtypeStruct((n, CONV_DIM), F32)
    else:
        p_spec = pl.BlockSpec((8, CONV_DIM), lambda i: (i // tiles_per_seq, 0))
        p_shape = jax.ShapeDtypeStruct((n // seq_rows * 8, CONV_DIM), F32)
        scratch.append(pltpu.VMEM((8, CONV_DIM), F32))
    return pl.pallas_call(
        functools.partial(_conv_gate_kernel, sample=sample, tiles_per_seq=tiles_per_seq),
        grid=(n // tm,),
        in_specs=in_specs,
        out_specs=[row_spec, row_spec, p_spec],
        out_shape=[jax.ShapeDtypeStruct((n, D_MODEL), F32),
                   jax.ShapeDtypeStruct((n, D_MODEL), F32), p_shape],
        scratch_shapes=scratch,
        compiler_params=_params(1),
        name=name,
    )(*args)


def _attn_prompt_kernel(q_ref, kp_ref, kc_ref, vp_ref, vc_ref, o_ref, l_ref):
    n = pl.program_id(2)
    k = jnp.concatenate([kp_ref[...], kc_ref[...]], axis=0)
    v = jnp.concatenate([vp_ref[...], vc_ref[...]], axis=0)
    row = lax.broadcasted_iota(jnp.int32, (2 * QBLOCK, 2 * QBLOCK), 0)
    col = lax.broadcasted_iota(jnp.int32, (2 * QBLOCK, 2 * QBLOCK), 1)
    qi = row & (QBLOCK - 1)
    prev_lo = qi + jnp.where(n > 0, 0, QBLOCK)
    valid = ((col >= QBLOCK) & (col - QBLOCK <= qi)) | ((col < QBLOCK) & (col >= prev_lo))
    even = lax.broadcasted_iota(jnp.int32, (QBLOCK, 2 * HEAD_DIM), 1) < HEAD_DIM
    for j in range(HEADS // 2):
        sl = slice(2 * HEAD_DIM * j, 2 * HEAD_DIM * (j + 1))
        qp = q_ref[:, sl].astype(F32)
        lhs = jnp.concatenate([jnp.where(even, qp, 0.0), jnp.where(even, 0.0, qp)],
                              axis=0).astype(BF16)
        s = lax.dot_general(lhs, k[:, sl], (((1,), (1,)), ((), ())),
                            preferred_element_type=F32)
        s = jnp.where(valid, s, NEG)
        m = jnp.max(s, axis=1, keepdims=True)
        p = jnp.exp(s - m)
        l = jnp.sum(p, axis=1, keepdims=True)
        pv = jnp.dot(p.astype(BF16), v[:, sl], preferred_element_type=F32)
        o = pv / l
        lse = jnp.broadcast_to(m + jnp.log(l), o.shape)
        o_ref[:, sl] = jnp.where(even, o[:QBLOCK], o[QBLOCK:])
        l_ref[:, sl] = jnp.where(even, lse[:QBLOCK], lse[QBLOCK:])


def _attn_prompt_call(qkv, name):
    batch, dil, length, _ = qkv.shape
    nb = length // QBLOCK

    def spec(kind, prev):
        if prev:
            return pl.BlockSpec((None, None, QBLOCK, GROUP_COLS),
                                lambda b, r, n: (b, r, jnp.maximum(n - 1, 0), kind))
        return pl.BlockSpec((None, None, QBLOCK, GROUP_COLS), lambda b, r, n: (b, r, n, kind))

    out_spec = pl.BlockSpec((None, None, QBLOCK, GROUP_COLS), lambda b, r, n: (b, r, n, 0))
    out_shape = jax.ShapeDtypeStruct((batch, dil, length, GROUP_COLS), F32)
    return pl.pallas_call(
        _attn_prompt_kernel,
        grid=(batch, dil, nb),
        in_specs=[spec(0, False), spec(1, True), spec(1, False), spec(2, True), spec(2, False)],
        out_specs=[out_spec, out_spec],
        out_shape=[out_shape, out_shape],
        compiler_params=_params(3),
        name=name,
    )(qkv, qkv, qkv, qkv, qkv)


def _attn_sample_kernel(*refs, lb, dil, dec, aliased):
    if aliased:
        q_ref, kn_ref, vn_ref, c_ref, _, o_ref, l_ref, nc_ref = refs
    else:
        q_ref, kn_ref, vn_ref, c_ref, o_ref, l_ref, nc_ref = refs
    q = q_ref[...]
    sub = lax.broadcasted_iota(jnp.int32, (SAMPLE_ROWS, GROUP_COLS), 0)
    lane = lax.broadcasted_iota(jnp.int32, (SAMPLE_ROWS, GROUP_COLS), 1)
    diag = sub == lane // HEAD_DIM
    qb = jnp.concatenate(
        [jnp.where(diag, jnp.broadcast_to(q[t:t + 1], diag.shape), 0.0) for t in range(dec)],
        axis=0).astype(BF16)
    rows = dec * HEADS
    kt = c_ref[0].astype(BF16)
    vt = c_ref[1].astype(BF16)
    s = jnp.dot(qb, kt, preferred_element_type=F32)
    t_of = lax.broadcasted_iota(jnp.int32, (rows, lb), 0) // HEADS
    diff = lax.broadcasted_iota(jnp.int32, (rows, lb), 1) - t_of
    valid = (diff >= 0) & ((diff & (dil - 1)) == 0)
    s = jnp.where(valid, s, NEG)
    m = jnp.max(s, axis=1, keepdims=True)

    qbf = qb.astype(F32)
    knr = kn_ref[...].astype(BF16).astype(F32)
    vnr = vn_ref[...].astype(BF16).astype(F32)
    t_col = lax.broadcasted_iota(jnp.int32, (rows, 1), 0) // HEADS
    s_new = []
    for t2 in range(dec):
        s2 = jnp.sum(qbf * knr[t2:t2 + 1], axis=1, keepdims=True)
        d2 = t_col - t2
        s2 = jnp.where((d2 >= 0) & ((d2 & (dil - 1)) == 0), s2, NEG)
        s_new.append(s2)
        m = jnp.maximum(m, s2)
    p = jnp.exp(s - m)
    l = jnp.sum(p, axis=1, keepdims=True)
    acc = lax.dot_general(p.astype(BF16), vt, (((1,), (1,)), ((), ())),
                          preferred_element_type=F32)
    for t2 in range(dec):
        pn = jnp.exp(s_new[t2] - m)
        l = l + pn
        acc = acc + pn * vnr[t2:t2 + 1]
    o = acc / l
    lse = jnp.broadcast_to(m + jnp.log(l), o.shape)
    o_rows, l_rows = [], []
    for t in range(dec):
        blk = slice(t * HEADS, (t + 1) * HEADS)
        o_rows.append(jnp.sum(jnp.where(diag, o[blk], 0.0), axis=0, keepdims=True))
        l_rows.append(jnp.sum(jnp.where(diag, lse[blk], 0.0), axis=0, keepdims=True))
    pad = jnp.zeros((SAMPLE_ROWS - dec, GROUP_COLS), F32)
    o_ref[...] = jnp.concatenate(o_rows + [pad], axis=0)
    l_ref[...] = jnp.concatenate(l_rows + [pad], axis=0)

    lane_t = lax.broadcasted_iota(jnp.int32, (GROUP_COLS, 128), 1)
    for kv, new_ref in ((0, kn_ref), (1, vn_ref)):
        rolled = pltpu.roll(c_ref[kv], lb - dec, axis=1)
        nc_ref[kv] = rolled
        shifted = jnp.where(sub >= SAMPLE_ROWS - dec, pltpu.roll(new_ref[...], SAMPLE_ROWS - dec, axis=0), 0.0)
        padded = jnp.concatenate([jnp.zeros((128 - SAMPLE_ROWS, GROUP_COLS), F32), shifted], axis=0)
        new_t = padded.T
        nc_ref[kv, :, lb - 128:] = jnp.where(lane_t >= 128 - dec, new_t, rolled[:, lb - 128:])


def _attn_sample_call(qkv32, cache, prev_out, layer, g, dil, dec, name):
    depth, n_seq, _, _, lb = cache.shape
    n = qkv32.shape[0]
    row = lambda off: pl.BlockSpec((SAMPLE_ROWS, GROUP_COLS), lambda s: (s, off + g))
    cache_spec = pl.BlockSpec((None, None, 2, GROUP_COLS, lb), lambda s: (layer, s, 0, 0, 0))
    in_specs = [row(0), row(N_GROUPS), row(2 * N_GROUPS), cache_spec]
    args = [qkv32, qkv32, qkv32, cache]
    aliases = {}
    if prev_out is not None:
        in_specs.append(pl.BlockSpec(memory_space=pl.ANY))
        args.append(prev_out)
        aliases = {4: 2}
    out_row = pl.BlockSpec((SAMPLE_ROWS, GROUP_COLS), lambda s: (s, 0))
    return pl.pallas_call(
        functools.partial(_attn_sample_kernel, lb=lb, dil=dil, dec=dec, aliased=prev_out is not None),
        grid=(n_seq,),
        in_specs=in_specs,
        out_specs=[out_row, out_row, cache_spec],
        out_shape=[jax.ShapeDtypeStruct((n, GROUP_COLS), F32),
                   jax.ShapeDtypeStruct((n, GROUP_COLS), F32),
                   jax.ShapeDtypeStruct(cache.shape, F32)],
        input_output_aliases=aliases,
        compiler_params=_params(1),
        name=name,
    )(*args)


def _mix_out_kernel(x_ref, *refs):
    attn_refs = refs[:2 * N_GROUPS]
    ga_ref, mc_ref, wao_ref, wo_ref, out_ref = refs[2 * N_GROUPS:2 * N_GROUPS + 5]
    scratch = list(refs[2 * N_GROUPS + 5:])
    tm = x_ref.shape[0]

    def natural(ref):
        dil = ref.shape[0]
        if dil == 1:
            return ref[0]
        s_ref = scratch.pop()
        for r in range(dil):
            for t in range(LANE_TILES):
                s_ref[t, pl.ds(r, tm // dil, stride=dil), :] = ref[r, :, t * LANES:(t + 1) * LANES]
        return jnp.concatenate([s_ref[t] for t in range(LANE_TILES)], axis=1)

    o = [natural(attn_refs[2 * g]) for g in range(N_GROUPS)]
    lse = [natural(attn_refs[2 * g + 1]) for g in range(N_GROUPS)]
    mx = jnp.maximum(jnp.maximum(lse[0], lse[1]), lse[2])
    e = [jnp.exp(l - mx) for l in lse]
    a = (e[0] * o[0] + e[1] * o[1] + e[2] * o[2]) / (e[0] + e[1] + e[2])
    t = jnp.dot(a.astype(BF16), wao_ref[...], preferred_element_type=F32)
    m = ga_ref[...] * t + mc_ref[...]
    out_ref[...] = x_ref[...] + jnp.dot(m.astype(BF16), wo_ref[...], preferred_element_type=F32)


def _mix_out_call(x, attn, ga, mc, wao, wo, layer, tm, seq_rows, name):
    n = x.shape[0]
    tiles_per_seq = seq_rows // tm
    wide = pl.BlockSpec((tm, D_MODEL), lambda i: (i, 0))
    flat = [a for pair in attn for a in pair]
    attn_specs = [pl.BlockSpec((None, a.shape[1], tm // a.shape[1], GROUP_COLS),
                               lambda i: (i // tiles_per_seq, 0, i % tiles_per_seq, 0))
                  for a in flat]
    n_scratch = sum(a.shape[1] > 1 for a in flat)
    return pl.pallas_call(
        _mix_out_kernel,
        grid=(n // tm,),
        in_specs=[wide] + attn_specs + [wide, wide,
                  _resident((None, GROUP_COLS, D_MODEL), lambda i: (layer, 0, 0)),
                  _resident((None, D_MODEL, D_MODEL), lambda i: (layer, 0, 0))],
        out_specs=wide,
        out_shape=jax.ShapeDtypeStruct((n, D_MODEL), F32),
        scratch_shapes=[pltpu.VMEM((LANE_TILES, tm, LANES), F32)] * n_scratch,
        compiler_params=_params(1),
        name=name,
    )(x, *flat, ga, mc, wao, wo)


def kernel(x_prompt, x_sample, cache_kv1, cache_kv2, cache_kv3, state_conv, ffn1_norm, ffn1_w_in,
           ffn1_w_out, mix_norm, w_in, b_gate, conv_w, w_attn_out, w_conv_out, w_out, ffn2_norm,
           ffn2_w_in, ffn2_w_out, final_norm):
    batch, seq, _ = x_prompt.shape
    n_seq, dec, _ = x_sample.shape
    depth = w_in.shape[0]
    caches = (cache_kv1, cache_kv2, cache_kv3)
    assert dec <= SAMPLE_ROWS - 2 and seq % (GROUPS[-1][1] * QBLOCK) == 0
    for cache, (win, dil) in zip(caches, GROUPS):
        assert cache.shape[2] == win and win == SPAN * dil and (dil == 1 or dec <= dil)

    f1_wi, f1_wo = ffn1_w_in.astype(BF16), ffn1_w_out.astype(BF16)
    f2_wi, f2_wo = ffn2_w_in.astype(BF16), ffn2_w_out.astype(BF16)
    w_qkv = w_in[:, :, :QKV_DIM].astype(BF16)
    w_conv = w_in[:, :, QKV_DIM:QKV_DIM + 3 * CONV_DIM].astype(BF16)
    w_gate = w_in[:, :, QKV_DIM + 3 * CONV_DIM:].astype(BF16)
    w_ao = w_attn_out.astype(BF16)
    w_co = w_conv_out.astype(BF16)
    w_o = w_out.astype(BF16)
    g1 = ffn1_norm.reshape(depth, 1, D_MODEL)
    gm = mix_norm.reshape(depth, 1, D_MODEL)
    g2 = ffn2_norm.reshape(depth, 1, D_MODEL)
    gf = final_norm.reshape(1, D_MODEL)
    bg = b_gate.reshape(depth, 1, 2 * D_MODEL)

    caches_t = [c.transpose(0, 1, 3, 4, 5, 2).reshape(depth, n_seq, 2, GROUP_COLS, c.shape[2])
                for c in caches]

    xp = x_prompt.reshape(batch * seq, D_MODEL)
    xs = jnp.pad(x_sample, ((0, 0), (0, SAMPLE_ROWS - dec), (0, 0))).reshape(n_seq * SAMPLE_ROWS, D_MODEL)
    ns = xs.shape[0]
    tail_rows = GROUPS[-1][0]

    kv_prompt = [[] for _ in GROUPS]
    conv_prompt, conv_sample = [], []
    new_caches = [None] * N_GROUPS
    for l in range(depth):
        last = l == depth - 1
        xs = _ffn_call(xs, g1, f1_wi, f1_wo, gf, l, ns, False, f"s_ffn1_{l}")
        s_tail = _qkv_call(xs, gm, w_qkv, l, ns, ns, ns, f"s_qkv_{l}")[-1]
        st = jnp.pad(state_conv[l], ((0, 0), (SAMPLE_ROWS - 2, 0), (0, 0)))
        st = jnp.roll(st, -1, axis=0).reshape(ns, CONV_DIM)
        s_ga, s_mc, s_p = _conv_gate_call(xs, gm, w_conv, w_gate, bg, conv_w, w_co, st, l, ns, ns,
                                          f"s_conv_{l}")
        conv_sample.append(s_p.reshape(n_seq, SAMPLE_ROWS, CONV_DIM)[:, dec - 2:dec])
        s_attn = []
        for g, (win, dil) in enumerate(GROUPS):
            o, lse, new_caches[g] = _attn_sample_call(s_tail, caches_t[g], new_caches[g], l, g, dil,
                                                      dec, f"s_attn{g}_{l}")
            s_attn.append((o.reshape(1, 1, ns, GROUP_COLS), lse.reshape(1, 1, ns, GROUP_COLS)))
        xs = _mix_out_call(xs, s_attn, s_ga, s_mc, w_ao, w_o, l, ns, ns, f"s_mix_{l}")
        xs = _ffn_call(xs, g2, f2_wi, f2_wo, gf, l, ns, last, f"s_ffn2_{l}")

        xp = _ffn_call(xp, g1, f1_wi, f1_wo, gf, l, ROW_TILE, False, f"p_ffn1_{l}")
        *qkv_groups, p_tail = _qkv_call(xp, gm, w_qkv, l, ROW_TILE, seq, tail_rows, f"p_qkv_{l}")
        p_ga, p_mc, p_p = _conv_gate_call(xp, gm, w_conv, w_gate, bg, conv_w, w_co, None, l,
                                          ROW_TILE, seq, f"p_conv_{l}")
        conv_prompt.append(p_p.reshape(batch, 8, CONV_DIM)[:, 6:8])
        p_tail = p_tail.reshape(batch, tail_rows, QKV_DIM)
        p_attn = []
        for g, (win, dil) in enumerate(GROUPS):
            p_attn.append(_attn_prompt_call(qkv_groups[g], f"p_attn{g}_{l}"))
            keep = min(win, seq)
            kcol = ATTN_DIM + g * GROUP_COLS
            vcol = 2 * ATTN_DIM + g * GROUP_COLS
            k_t = p_tail[:, tail_rows - keep:, kcol:kcol + GROUP_COLS]
            v_t = p_tail[:, tail_rows - keep:, vcol:vcol + GROUP_COLS]
            kv_prompt[g].append(jnp.stack([k_t, v_t], axis=2).reshape(batch, keep, 2, HEADS, HEAD_DIM))
        xp = _mix_out_call(xp, p_attn, p_ga, p_mc, w_ao, w_o, l, ROW_TILE, seq, f"p_mix_{l}")
        xp = _ffn_call(xp, g2, f2_wi, f2_wo, gf, l, ROW_TILE, last, f"p_ffn2_{l}")

    y_prompt = xp.reshape(batch, seq, D_MODEL)
    y_sample = xs.reshape(n_seq, SAMPLE_ROWS, D_MODEL)[:, :dec]
    kv_sample = [c.reshape(depth, n_seq, 2, HEADS, HEAD_DIM, c.shape[-1]).transpose(0, 1, 5, 2, 3, 4)
                 for c in new_caches]
    return (y_prompt, y_sample,
            jnp.stack(kv_prompt[0]), jnp.stack(kv_prompt[1]), jnp.stack(kv_prompt[2]),
            jnp.stack(conv_prompt),
            kv_sample[0], kv_sample[1], kv_sample[2],
            jnp.stack(conv_sample))
```

```python
import functools

import jax
import jax.numpy as jnp
from jax import lax
from jax.experimental import pallas as pl
from jax.experimental.pallas import tpu as pltpu

F32 = jnp.float32
BF16 = jnp.bfloat16

D_MODEL = 1024
D_FF = 2816
HEAD_DIM = 64
HEADS = 8
GROUP_COLS = HEADS * HEAD_DIM
GROUPS = ((128, 1), (512, 4), (2048, 16))
N_GROUPS = len(GROUPS)
SPAN = 128
ATTN_DIM = N_GROUPS * GROUP_COLS
QKV_DIM = 3 * ATTN_DIM
CONV_DIM = D_MODEL
RMS_EPS = 1e-6
ATTN_SCALE = HEAD_DIM ** -0.5
NEG = -1e30

FF_CHUNK = 256
N_FF_CHUNKS = D_FF // FF_CHUNK
QBLOCK = 128
ATTN_BLOCKS = 4
LANES = 128
LANE_TILES = GROUP_COLS // LANES
SAMPLE_ROWS = 8
V7X_VMEM_LIMIT_BYTES = 56 * 1024 * 1024
ROW_TILE = 512


def _params(n_axes):
    return pltpu.CompilerParams(
        dimension_semantics=("arbitrary",) * n_axes,
        vmem_limit_bytes=V7X_VMEM_LIMIT_BYTES)


def _resident(block_shape, index_map):
    return pl.BlockSpec(block_shape, index_map, pipeline_mode=pl.Buffered(1))


def _rms_inv(x):
    return lax.rsqrt(jnp.mean(x * x, axis=-1, keepdims=True) + RMS_EPS)


def _ffn_kernel(x_ref, g_ref, wi_ref, wo_ref, fg_ref, o_ref, h_s, acc_s, *, final):
    x = x_ref[...]
    h_s[...] = ((x * _rms_inv(x)) * g_ref[...]).astype(BF16)
    for c in range(N_FF_CHUNKS):
        lo = c * FF_CHUNK
        gate = jnp.dot(h_s[...], wi_ref[:, lo:lo + FF_CHUNK], preferred_element_type=F32)
        up = jnp.dot(h_s[...], wi_ref[:, D_FF + lo:D_FF + lo + FF_CHUNK],
                     preferred_element_type=F32)
        act = ((gate * jax.nn.sigmoid(gate)) * up).astype(BF16)
        part = jnp.dot(act, wo_ref[lo:lo + FF_CHUNK, :], preferred_element_type=F32)
        if c == 0:
            acc_s[...] = part
        else:
            acc_s[...] += part
    y = x_ref[...] + 0.5 * acc_s[...]
    if final:
        y = (y * _rms_inv(y)) * fg_ref[...]
    o_ref[...] = y


def _ffn_call(x, gains, wi, wo, final_gain, layer, tm, final, name):
    n = x.shape[0]
    return pl.pallas_call(
        functools.partial(_ffn_kernel, final=final),
        grid=(n // tm,),
        in_specs=[
            pl.BlockSpec((tm, D_MODEL), lambda i: (i, 0)),
            pl.BlockSpec((None, 1, D_MODEL), lambda i: (layer, 0, 0)),
            _resident((None, D_MODEL, 2 * D_FF), lambda i: (layer, 0, 0)),
            _resident((None, D_FF, D_MODEL), lambda i: (layer, 0, 0)),
            pl.BlockSpec((1, D_MODEL), lambda i: (0, 0)),
        ],
        out_specs=pl.BlockSpec((tm, D_MODEL), lambda i: (i, 0)),
        out_shape=jax.ShapeDtypeStruct((n, D_MODEL), F32),
        scratch_shapes=[pltpu.VMEM((tm, D_MODEL), BF16), pltpu.VMEM((tm, D_MODEL), F32)],
        compiler_params=_params(1),
        name=name,
    )(x, gains, wi, wo, final_gain)


def _qkv_kernel(x_ref, g_ref, w_ref, *refs):
    group_refs, tail_ref, u_s, z_s = refs[:N_GROUPS], refs[N_GROUPS], refs[N_GROUPS + 1], refs[N_GROUPS + 2]
    tm = x_ref.shape[0]
    x = x_ref[...]
    u_s[...] = ((x * _rms_inv(x)) * g_ref[...]).astype(BF16)
    for c in range(QKV_DIM // GROUP_COLS):
        kind, g = divmod(c, N_GROUPS)
        sl = slice(c * GROUP_COLS, (c + 1) * GROUP_COLS)
        z = jnp.dot(u_s[...], w_ref[:, sl], preferred_element_type=F32)
        if kind == 0:
            z = z * ATTN_SCALE
        tail_ref[:, sl] = z
        out = group_refs[g]
        dil = out.shape[0]
        osl = slice(kind * GROUP_COLS, (kind + 1) * GROUP_COLS)
        if dil == 1:
            out[0, :, osl] = z.astype(BF16)
        else:
            for t in range(LANE_TILES):
                z_s[t] = z[:, t * LANES:(t + 1) * LANES]
            for r in range(dil):
                for t in range(LANE_TILES):
                    lo = kind * GROUP_COLS + t * LANES
                    out[r, :, lo:lo + LANES] = z_s[t, pl.ds(r, tm // dil, stride=dil), :].astype(BF16)


def _qkv_call(x, gains, w, layer, tm, seq_rows, tail_rows, name):
    n = x.shape[0]
    tiles_per_seq = seq_rows // tm
    tail_tiles = tail_rows // tm
    n_seq = n // seq_rows

    def tail_map(i):
        j = jnp.maximum(i % tiles_per_seq - (tiles_per_seq - tail_tiles), 0)
        return ((i // tiles_per_seq) * tail_tiles + j, 0)

    group_specs = [pl.BlockSpec((None, dil, tm // dil, 3 * GROUP_COLS),
                                lambda i: (i // tiles_per_seq, 0, i % tiles_per_seq, 0))
                   for _, dil in GROUPS]
    group_shapes = [jax.ShapeDtypeStruct((n_seq, dil, seq_rows // dil, 3 * GROUP_COLS), BF16)
                    for _, dil in GROUPS]
    return pl.pallas_call(
        _qkv_kernel,
        grid=(n // tm,),
        in_specs=[
            pl.BlockSpec((tm, D_MODEL), lambda i: (i, 0)),
            pl.BlockSpec((None, 1, D_MODEL), lambda i: (layer, 0, 0)),
            _resident((None, D_MODEL, QKV_DIM), lambda i: (layer, 0, 0)),
        ],
        out_specs=group_specs + [pl.BlockSpec((tm, QKV_DIM), tail_map)],
        out_shape=group_shapes + [jax.ShapeDtypeStruct((n_seq * tail_rows, QKV_DIM), F32)],
        scratch_shapes=[pltpu.VMEM((tm, D_MODEL), BF16), pltpu.VMEM((LANE_TILES, tm, LANES), F32)],
        compiler_params=_params(1),
        name=name,
    )(x, gains, w)


CONV_CHUNK = 256
GATE_CHUNK = 512


def _mixer_kernel(*refs, sample, tiles_per_seq):
    x_ref, g_ref, wc_ref, wg_ref, bg_ref, cw_ref, wco_ref, wao_ref, wo_ref = refs[:9]
    refs = refs[9:]
    if sample:
        st_ref, refs = refs[0], refs[1:]
    attn_refs, refs = refs[:2 * N_GROUPS], refs[2 * N_GROUPS:]
    out_ref, p_ref, u_s, c_s, m_s = refs[:5]
    scratch = list(refs[5:])
    if not sample:
        halo_s = scratch.pop(0)
    tm = x_ref.shape[0]
    x = x_ref[...]
    u_s[...] = ((x * _rms_inv(x)) * g_ref[...]).astype(BF16)
    row = lax.broadcasted_iota(jnp.int32, (tm, CONV_CHUNK), 0)

    if not sample:
        @pl.when(pl.program_id(0) % tiles_per_seq == 0)
        def _():
            halo_s[...] = jnp.zeros_like(halo_s)

    for c in range(CONV_DIM // CONV_CHUNK):
        sl = slice(c * CONV_CHUNK, (c + 1) * CONV_CHUNK)
        u = u_s[...]
        cb = jnp.dot(u, wc_ref[:, sl], preferred_element_type=F32)
        cc = jnp.dot(u, wc_ref[:, CONV_DIM + c * CONV_CHUNK:CONV_DIM + (c + 1) * CONV_CHUNK],
                     preferred_element_type=F32)
        ch = jnp.dot(u, wc_ref[:, 2 * CONV_DIM + c * CONV_CHUNK:2 * CONV_DIM + (c + 1) * CONV_CHUNK],
                     preferred_element_type=F32)
        p = cc * ch
        if sample:
            p_ref[:, sl] = p
            p = jnp.where((row & (SAMPLE_ROWS - 1)) >= SAMPLE_ROWS - 2, st_ref[:, sl], p)
            r1 = pltpu.roll(p, 1, axis=0)
            r2 = pltpu.roll(p, 2, axis=0)
        else:
            r1 = pltpu.roll(p, 1, axis=0)
            r2 = pltpu.roll(p, 2, axis=0)
            h = halo_s[:, sl]
            r1 = jnp.where(row == 0, h[7:8], r1)
            r2 = jnp.where(row == 0, h[6:7], jnp.where(row == 1, h[7:8], r2))
            halo_s[:, sl] = p[tm - 8:tm]
            p_ref[:, sl] = p[tm - 8:tm]
        conv = cw_ref[0:1, sl] * r2 + cw_ref[1:2, sl] * r1 + cw_ref[2:3, sl] * p
        c_s[:, sl] = (cb * conv).astype(BF16)

    def natural(ref):
        dil = ref.shape[0]
        if dil == 1:
            return ref[0]
        s_ref = scratch.pop()
        for r in range(dil):
            for t in range(LANE_TILES):
                s_ref[t, pl.ds(r, tm // dil, stride=dil), :] = ref[r, :, t * LANES:(t + 1) * LANES]
        return jnp.concatenate([s_ref[t] for t in range(LANE_TILES)], axis=1)

    o = [natural(attn_refs[2 * g]) for g in range(N_GROUPS)]
    lse = [natural(attn_refs[2 * g + 1]) for g in range(N_GROUPS)]
    mx = jnp.maximum(jnp.maximum(lse[0], lse[1]), lse[2])
    e = [jnp.exp(l - mx) for l in lse]
    a = (e[0] * o[0] + e[1] * o[1] + e[2] * o[2]) / (e[0] + e[1] + e[2])
    aproj = jnp.dot(a.astype(BF16), wao_ref[...], preferred_element_type=F32)
    cproj = jnp.dot(c_s[...], wco_ref[...], preferred_element_type=F32)
    for c in range(D_MODEL // GATE_CHUNK):
        sl = slice(c * GATE_CHUNK, (c + 1) * GATE_CHUNK)
        za = jnp.dot(u_s[...], wg_ref[:, sl], preferred_element_type=F32) + bg_ref[:, sl]
        zc = (jnp.dot(u_s[...], wg_ref[:, D_MODEL + c * GATE_CHUNK:D_MODEL + (c + 1) * GATE_CHUNK],
                      preferred_element_type=F32)
              + bg_ref[:, D_MODEL + c * GATE_CHUNK:D_MODEL + (c + 1) * GATE_CHUNK])
        m = jax.nn.sigmoid(za) * aproj[:, sl] + jax.nn.sigmoid(zc) * cproj[:, sl]
        m_s[:, sl] = m.astype(BF16)
    out_ref[...] = x_ref[...] + jnp.dot(m_s[...], wo_ref[...], preferred_element_type=F32)


def _mixer_call(x, gains, wc, wg, bg, cw, wco, wao, wo, state, attn, layer, tm, seq_rows, name):
    n = x.shape[0]
    sample = state is not None
    tiles_per_seq = seq_rows // tm
    row_spec = pl.BlockSpec((tm, D_MODEL), lambda i: (i, 0))
    in_specs = [
        row_spec,
        pl.BlockSpec((None, 1, D_MODEL), lambda i: (layer, 0, 0)),
        _resident((None, D_MODEL, 3 * CONV_DIM), lambda i: (layer, 0, 0)),
        _resident((None, D_MODEL, 2 * D_MODEL), lambda i: (layer, 0, 0)),
        pl.BlockSpec((None, 1, 2 * D_MODEL), lambda i: (layer, 0, 0)),
        pl.BlockSpec((None, 3, CONV_DIM), lambda i: (layer, 0, 0)),
        _resident((None, CONV_DIM, D_MODEL), lambda i: (layer, 0, 0)),
        _resident((None, GROUP_COLS, D_MODEL), lambda i: (layer, 0, 0)),
        _resident((None, D_MODEL, D_MODEL), lambda i: (layer, 0, 0)),
    ]
    args = [x, gains, wc, wg, bg, cw, wco, wao, wo]
    scratch = [pltpu.VMEM((tm, D_MODEL), BF16), pltpu.VMEM((tm, CONV_DIM), BF16),
               pltpu.VMEM((tm, D_MODEL), BF16)]
    if sample:
        in_specs.append(row_spec)
        args.append(state)
        p_spec = row_spec
        p_shape = jax.ShapeDtypeStruct((n, CONV_DIM), F32)
    else:
        p_spec = pl.BlockSpec((8, CONV_DIM), lambda i: (i // tiles_per_seq, 0))
        p_shape = jax.ShapeDtypeStruct((n // seq_rows * 8, CONV_DIM), F32)
        scratch.append(pltpu.VMEM((8, CONV_DIM), F32))
    flat = [a for pair in attn for a in pair]
    in_specs += [pl.BlockSpec((None, a.shape[1], tm // a.shape[1], GROUP_COLS),
                              lambda i: (i // tiles_per_seq, 0, i % tiles_per_seq, 0))
                 for a in flat]
    scratch += [pltpu.VMEM((LANE_TILES, tm, LANES), F32)] * sum(a.shape[1] > 1 for a in flat)
    return pl.pallas_call(
        functools.partial(_mixer_kernel, sample=sample, tiles_per_seq=tiles_per_seq),
        grid=(n // tm,),
        in_specs=in_specs,
        out_specs=[row_spec, p_spec],
        out_shape=[jax.ShapeDtypeStruct((n, D_MODEL), F32), p_shape],
        scratch_shapes=scratch,
        compiler_params=_params(1),
        name=name,
    )(*args, *flat)


def _attn_prompt_kernel(q_ref, kp_ref, kc_ref, vp_ref, vc_ref, o_ref, l_ref):
    n = pl.program_id(2)
    row = lax.broadcasted_iota(jnp.int32, (2 * QBLOCK, 2 * QBLOCK), 0)
    col = lax.broadcasted_iota(jnp.int32, (2 * QBLOCK, 2 * QBLOCK), 1)
    qi = row & (QBLOCK - 1)
    cur_ok = (col >= QBLOCK) & (col - QBLOCK <= qi)
    valid_inner = cur_ok | ((col < QBLOCK) & (col >= qi))
    valid_first = cur_ok | ((col < QBLOCK) & (col >= qi + jnp.where(n > 0, 0, QBLOCK)))
    even = lax.broadcasted_iota(jnp.int32, (QBLOCK, 2 * HEAD_DIM), 1) < HEAD_DIM
    for i in range(ATTN_BLOCKS):
        rows = slice(i * QBLOCK, (i + 1) * QBLOCK)
        if i == 0:
            k = jnp.concatenate([kp_ref[...], kc_ref[rows]], axis=0)
            v = jnp.concatenate([vp_ref[...], vc_ref[rows]], axis=0)
            valid = valid_first
        else:
            k = kc_ref[(i - 1) * QBLOCK:(i + 1) * QBLOCK]
            v = vc_ref[(i - 1) * QBLOCK:(i + 1) * QBLOCK]
            valid = valid_inner
        for j in range(HEADS // 2):
            sl = slice(2 * HEAD_DIM * j, 2 * HEAD_DIM * (j + 1))
            qp = q_ref[rows, sl].astype(F32)
            lhs = jnp.concatenate([jnp.where(even, qp, 0.0), jnp.where(even, 0.0, qp)],
                                  axis=0).astype(BF16)
            s = lax.dot_general(lhs, k[:, sl], (((1,), (1,)), ((), ())),
                                preferred_element_type=F32)
            s = jnp.where(valid, s, NEG)
            m = jnp.max(s, axis=1, keepdims=True)
            p = jnp.exp(s - m)
            l = jnp.sum(p, axis=1, keepdims=True)
            pv = jnp.dot(p.astype(BF16), v[:, sl], preferred_element_type=F32)
            o = pv / l
            lse = jnp.broadcast_to(m + jnp.log(l), o.shape)
            o_ref[rows, sl] = jnp.where(even, o[:QBLOCK], o[QBLOCK:])
            l_ref[rows, sl] = jnp.where(even, lse[:QBLOCK], lse[QBLOCK:])


def _attn_prompt_call(qkv, name):
    batch, dil, length, _ = qkv.shape
    step_rows = ATTN_BLOCKS * QBLOCK
    nb = length // step_rows

    def spec(kind, prev):
        if prev:
            return pl.BlockSpec((None, None, QBLOCK, GROUP_COLS),
                                lambda b, r, n: (b, r, jnp.maximum(ATTN_BLOCKS * n - 1, 0), kind))
        return pl.BlockSpec((None, None, step_rows, GROUP_COLS), lambda b, r, n: (b, r, n, kind))

    out_spec = pl.BlockSpec((None, None, step_rows, GROUP_COLS), lambda b, r, n: (b, r, n, 0))
    out_shape = jax.ShapeDtypeStruct((batch, dil, length, GROUP_COLS), F32)
    return pl.pallas_call(
        _attn_prompt_kernel,
        grid=(batch, dil, nb),
        in_specs=[spec(0, False), spec(1, True), spec(1, False), spec(2, True), spec(2, False)],
        out_specs=[out_spec, out_spec],
        out_shape=[out_shape, out_shape],
        compiler_params=_params(3),
        name=name,
    )(qkv, qkv, qkv, qkv, qkv)


def _attn_sample_kernel(*refs, lb, dil, dec, aliased):
    if aliased:
        q_ref, kn_ref, vn_ref, c_ref, _, o_ref, l_ref, nc_ref = refs
    else:
        q_ref, kn_ref, vn_ref, c_ref, o_ref, l_ref, nc_ref = refs
    q = q_ref[...]
    sub = lax.broadcasted_iota(jnp.int32, (SAMPLE_ROWS, GROUP_COLS), 0)
    lane = lax.broadcasted_iota(jnp.int32, (SAMPLE_ROWS, GROUP_COLS), 1)
    diag = sub == lane // HEAD_DIM
    qb = jnp.concatenate(
        [jnp.where(diag, jnp.broadcast_to(q[t:t + 1], diag.shape), 0.0) for t in range(dec)],
        axis=0).astype(BF16)
    rows = dec * HEADS
    kt = c_ref[0].astype(BF16)
    vt = c_ref[1].astype(BF16)
    s = jnp.dot(qb, kt, preferred_element_type=F32)
    t_of = lax.broadcasted_iota(jnp.int32, (rows, lb), 0) // HEADS
    diff = lax.broadcasted_iota(jnp.int32, (rows, lb), 1) - t_of
    valid = (diff >= 0) & ((diff & (dil - 1)) == 0)
    s = jnp.where(valid, s, NEG)
    m = jnp.max(s, axis=1, keepdims=True)

    qbf = qb.astype(F32)
    knr = kn_ref[...].astype(BF16).astype(F32)
    vnr = vn_ref[...].astype(BF16).astype(F32)
    t_col = lax.broadcasted_iota(jnp.int32, (rows, 1), 0) // HEADS
    s_new = []
    for t2 in range(dec):
        s2 = jnp.sum(qbf * knr[t2:t2 + 1], axis=1, keepdims=True)
        d2 = t_col - t2
        s2 = jnp.where((d2 >= 0) & ((d2 & (dil - 1)) == 0), s2, NEG)
        s_new.append(s2)
        m = jnp.maximum(m, s2)
    p = jnp.exp(s - m)
    l = jnp.sum(p, axis=1, keepdims=True)
    acc = lax.dot_general(p.astype(BF16), vt, (((1,), (1,)), ((), ())),
                          preferred_element_type=F32)
    for t2 in range(dec):
        pn = jnp.exp(s_new[t2] - m)
        l = l + pn
        acc = acc + pn * vnr[t2:t2 + 1]
    o = acc / l
    lse = jnp.broadcast_to(m + jnp.log(l), o.shape)
    o_rows, l_rows = [], []
    for t in range(dec):
        blk = slice(t * HEADS, (t + 1) * HEADS)
        o_rows.append(jnp.sum(jnp.where(diag, o[blk], 0.0), axis=0, keepdims=True))
        l_rows.append(jnp.sum(jnp.where(diag, lse[blk], 0.0), axis=0, keepdims=True))
    pad = jnp.zeros((SAMPLE_ROWS - dec, GROUP_COLS), F32)
    o_ref[...] = jnp.concatenate(o_rows + [pad], axis=0)
    l_ref[...] = jnp.concatenate(l_rows + [pad], axis=0)

    lane_t = lax.broadcasted_iota(jnp.int32, (GROUP_COLS, 128), 1)
    for kv, new_ref in ((0, kn_ref), (1, vn_ref)):
        rolled = pltpu.roll(c_ref[kv], lb - dec, axis=1)
        nc_ref[kv] = rolled
        shifted = jnp.where(sub >= SAMPLE_ROWS - dec, pltpu.roll(new_ref[...], SAMPLE_ROWS - dec, axis=0), 0.0)
        padded = jnp.concatenate([jnp.zeros((128 - SAMPLE_ROWS, GROUP_COLS), F32), shifted], axis=0)
        new_t = padded.T
        nc_ref[kv, :, lb - 128:] = jnp.where(lane_t >= 128 - dec, new_t, rolled[:, lb - 128:])


def _attn_sample_call(qkv32, cache, prev_out, layer, g, dil, dec, name):
    depth, n_seq, _, _, lb = cache.shape
    n = qkv32.shape[0]
    row = lambda off: pl.BlockSpec((SAMPLE_ROWS, GROUP_COLS), lambda s: (s, off + g))
    cache_spec = pl.BlockSpec((None, None, 2, GROUP_COLS, lb), lambda s: (layer, s, 0, 0, 0))
    in_specs = [row(0), row(N_GROUPS), row(2 * N_GROUPS), cache_spec]
    args = [qkv32, qkv32, qkv32, cache]
    aliases = {}
    if prev_out is not None:
        in_specs.append(pl.BlockSpec(memory_space=pl.ANY))
        args.append(prev_out)
        aliases = {4: 2}
    out_row = pl.BlockSpec((SAMPLE_ROWS, GROUP_COLS), lambda s: (s, 0))
    return pl.pallas_call(
        functools.partial(_attn_sample_kernel, lb=lb, dil=dil, dec=dec, aliased=prev_out is not None),
        grid=(n_seq,),
        in_specs=in_specs,
        out_specs=[out_row, out_row, cache_spec],
        out_shape=[jax.ShapeDtypeStruct((n, GROUP_COLS), F32),
                   jax.ShapeDtypeStruct((n, GROUP_COLS), F32),
                   jax.ShapeDtypeStruct(cache.shape, F32)],
        input_output_aliases=aliases,
        compiler_params=_params(1),
        name=name,
    )(*args)


def kernel(x_prompt, x_sample, cache_kv1, cache_kv2, cache_kv3, state_conv, ffn1_norm, ffn1_w_in,
           ffn1_w_out, mix_norm, w_in, b_gate, conv_w, w_attn_out, w_conv_out, w_out, ffn2_norm,
           ffn2_w_in, ffn2_w_out, final_norm):
    batch, seq, _ = x_prompt.shape
    n_seq, dec, _ = x_sample.shape
    depth = w_in.shape[0]
    caches = (cache_kv1, cache_kv2, cache_kv3)
    assert dec <= SAMPLE_ROWS - 2 and seq % (GROUPS[-1][1] * QBLOCK * ATTN_BLOCKS) == 0
    for cache, (win, dil) in zip(caches, GROUPS):
        assert cache.shape[2] == win and win == SPAN * dil and (dil == 1 or dec <= dil)

    f1_wi, f1_wo = ffn1_w_in.astype(BF16), ffn1_w_out.astype(BF16)
    f2_wi, f2_wo = ffn2_w_in.astype(BF16), ffn2_w_out.astype(BF16)
    w_qkv = w_in[:, :, :QKV_DIM].astype(BF16)
    w_conv = w_in[:, :, QKV_DIM:QKV_DIM + 3 * CONV_DIM].astype(BF16)
    w_gate = w_in[:, :, QKV_DIM + 3 * CONV_DIM:].astype(BF16)
    w_ao = w_attn_out.astype(BF16)
    w_co = w_conv_out.astype(BF16)
    w_o = w_out.astype(BF16)
    g1 = ffn1_norm.reshape(depth, 1, D_MODEL)
    gm = mix_norm.reshape(depth, 1, D_MODEL)
    g2 = ffn2_norm.reshape(depth, 1, D_MODEL)
    gf = final_norm.reshape(1, D_MODEL)
    bg = b_gate.reshape(depth, 1, 2 * D_MODEL)

    caches_t = [c.transpose(0, 1, 3, 4, 5, 2).reshape(depth, n_seq, 2, GROUP_COLS, c.shape[2])
                for c in caches]

    xp = x_prompt.reshape(batch * seq, D_MODEL)
    xs = jnp.pad(x_sample, ((0, 0), (0, SAMPLE_ROWS - dec), (0, 0))).reshape(n_seq * SAMPLE_ROWS, D_MODEL)
    ns = xs.shape[0]
    tail_rows = GROUPS[-1][0]

    kv_prompt = [[] for _ in GROUPS]
    conv_prompt, conv_sample = [], []
    new_caches = [None] * N_GROUPS
    for l in range(depth):
        last = l == depth - 1
        xs = _ffn_call(xs, g1, f1_wi, f1_wo, gf, l, ns, False, f"s_ffn1_{l}")
        s_tail = _qkv_call(xs, gm, w_qkv, l, ns, ns, ns, f"s_qkv_{l}")[-1]
        st = jnp.pad(state_conv[l], ((0, 0), (SAMPLE_ROWS - 2, 0), (0, 0)))
        st = jnp.roll(st, -1, axis=0).reshape(ns, CONV_DIM)
        s_attn = []
        for g, (win, dil) in enumerate(GROUPS):
            o, lse, new_caches[g] = _attn_sample_call(s_tail, caches_t[g], new_caches[g], l, g, dil,
                                                      dec, f"s_attn{g}_{l}")
            s_attn.append((o.reshape(1, 1, ns, GROUP_COLS), lse.reshape(1, 1, ns, GROUP_COLS)))
        xs, s_p = _mixer_call(xs, gm, w_conv, w_gate, bg, conv_w, w_co, w_ao, w_o, st, s_attn, l,
                              ns, ns, f"s_mix_{l}")
        conv_sample.append(s_p.reshape(n_seq, SAMPLE_ROWS, CONV_DIM)[:, dec - 2:dec])
        xs = _ffn_call(xs, g2, f2_wi, f2_wo, gf, l, ns, last, f"s_ffn2_{l}")

        xp = _ffn_call(xp, g1, f1_wi, f1_wo, gf, l, ROW_TILE, False, f"p_ffn1_{l}")
        *qkv_groups, p_tail = _qkv_call(xp, gm, w_qkv, l, ROW_TILE, seq, tail_rows, f"p_qkv_{l}")
        p_tail = p_tail.reshape(batch, tail_rows, QKV_DIM)
        p_attn = []
        for g, (win, dil) in enumerate(GROUPS):
            p_attn.append(_attn_prompt_call(qkv_groups[g], f"p_attn{g}_{l}"))
            keep = min(win, seq)
            kcol = ATTN_DIM + g * GROUP_COLS
            vcol = 2 * ATTN_DIM + g * GROUP_COLS
            k_t = p_tail[:, tail_rows - keep:, kcol:kcol + GROUP_COLS]
            v_t = p_tail[:, tail_rows - keep:, vcol:vcol + GROUP_COLS]
            kv_prompt[g].append(jnp.stack([k_t, v_t], axis=2).reshape(batch, keep, 2, HEADS, HEAD_DIM))
        xp, p_p = _mixer_call(xp, gm, w_conv, w_gate, bg, conv_w, w_co, w_ao, w_o, None, p_attn, l,
                              ROW_TILE, seq, f"p_mix_{l}")
        conv_prompt.append(p_p.reshape(batch, 8, CONV_DIM)[:, 6:8])
        xp = _ffn_call(xp, g2, f2_wi, f2_wo, gf, l, ROW_TILE, last, f"p_ffn2_{l}")

    y_prompt = xp.reshape(batch, seq, D_MODEL)
    y_sample = xs.reshape(n_seq, SAMPLE_ROWS, D_MODEL)[:, :dec]
    kv_sample = [c.reshape(depth, n_seq, 2, HEADS, HEAD_DIM, c.shape[-1]).transpose(0, 1, 5, 2, 3, 4)
                 for c in new_caches]
    return (y_prompt, y_sample,
            jnp.stack(kv_prompt[0]), jnp.stack(kv_prompt[1]), jnp.stack(kv_prompt[2]),
            jnp.stack(conv_prompt),
            kv_sample[0], kv_sample[1], kv_sample[2],
            jnp.stack(conv_sample))
```

```python
import functools

import jax
import jax.numpy as jnp
from jax import lax
from jax.experimental import pallas as pl
from jax.experimental.pallas import tpu as pltpu

F32 = jnp.float32
BF16 = jnp.bfloat16

D_MODEL = 1024
D_FF = 2816
HEAD_DIM = 64
HEADS = 8
GROUP_COLS = HEADS * HEAD_DIM
GROUPS = ((128, 1), (512, 4), (2048, 16))
N_GROUPS = len(GROUPS)
SPAN = 128
ATTN_DIM = N_GROUPS * GROUP_COLS
QKV_DIM = 3 * ATTN_DIM
CONV_DIM = D_MODEL
RMS_EPS = 1e-6
ATTN_SCALE = HEAD_DIM ** -0.5
NEG = -1e30

FF_CHUNK = 256
N_FF_CHUNKS = D_FF // FF_CHUNK
QBLOCK = 128
ATTN_BLOCKS = 4
LANES = 128
LANE_TILES = GROUP_COLS // LANES
SAMPLE_ROWS = 8
V7X_VMEM_LIMIT_BYTES = 56 * 1024 * 1024
ROW_TILE = 512


def _params(n_axes):
    return pltpu.CompilerParams(
        dimension_semantics=("arbitrary",) * n_axes,
        vmem_limit_bytes=V7X_VMEM_LIMIT_BYTES)


def _resident(block_shape, index_map):
    return pl.BlockSpec(block_shape, index_map, pipeline_mode=pl.Buffered(1))


def _rms_inv(x):
    return lax.rsqrt(jnp.mean(x * x, axis=-1, keepdims=True) + RMS_EPS)


def _roll_window(c_ref, new_rows, out_ref, dec):
    lb = c_ref.shape[1]
    rolled = pltpu.roll(c_ref[...], lb - dec, axis=1)
    out_ref[...] = rolled
    sub = lax.broadcasted_iota(jnp.int32, new_rows.shape, 0)
    shifted = jnp.where(sub >= SAMPLE_ROWS - dec,
                        pltpu.roll(new_rows, SAMPLE_ROWS - dec, axis=0), 0.0)
    padded = jnp.concatenate([jnp.zeros((LANES - SAMPLE_ROWS, GROUP_COLS), F32), shifted], axis=0)
    new_t = padded.T
    lane_t = lax.broadcasted_iota(jnp.int32, new_t.shape, 1)
    out_ref[:, lb - LANES:] = jnp.where(lane_t >= LANES - dec, new_t, rolled[:, lb - LANES:])


def _ffn_kernel(*refs, final, dec, n_alias):
    x_ref, g_ref, wi_ref, wo_ref, fg_ref = refs[:5]
    refs = refs[5:]
    if dec:
        new_ref, cache_refs = refs[0], refs[1:1 + N_GROUPS]
        refs = refs[1 + N_GROUPS + n_alias:]
        o_ref, roll_refs, refs = refs[0], refs[1:1 + N_GROUPS], refs[1 + N_GROUPS:]
    else:
        o_ref, refs = refs[0], refs[1:]
    h_s, acc_s = refs
    if dec:
        is_v = (lax.broadcasted_iota(jnp.int32, (SAMPLE_ROWS, GROUP_COLS), 0) * 0
                + pl.program_id(0) % 2) == 1
        for g in range(N_GROUPS):
            kcol = ATTN_DIM + g * GROUP_COLS
            vcol = 2 * ATTN_DIM + g * GROUP_COLS
            new_rows = jnp.where(is_v, new_ref[:, vcol:vcol + GROUP_COLS],
                                 new_ref[:, kcol:kcol + GROUP_COLS])
            _roll_window(cache_refs[g], new_rows, roll_refs[g], dec)
    x = x_ref[...]
    h_s[...] = ((x * _rms_inv(x)) * g_ref[...]).astype(BF16)
    for c in range(N_FF_CHUNKS):
        lo = c * FF_CHUNK
        gate = jnp.dot(h_s[...], wi_ref[:, lo:lo + FF_CHUNK], preferred_element_type=F32)
        up = jnp.dot(h_s[...], wi_ref[:, D_FF + lo:D_FF + lo + FF_CHUNK],
                     preferred_element_type=F32)
        act = ((gate * jax.nn.sigmoid(gate)) * up).astype(BF16)
        part = jnp.dot(act, wo_ref[lo:lo + FF_CHUNK, :], preferred_element_type=F32)
        if c == 0:
            acc_s[...] = part
        else:
            acc_s[...] += part
    y = x_ref[...] + 0.5 * acc_s[...]
    if final:
        y = (y * _rms_inv(y)) * fg_ref[...]
    o_ref[...] = y


def _ffn_call(x, gains, wi, wo, final_gain, layer, tm, final, name, roll=None):
    n = x.shape[0]
    in_specs = [
        pl.BlockSpec((tm, D_MODEL), lambda i: (i, 0)),
        pl.BlockSpec((None, 1, D_MODEL), lambda i: (layer, 0, 0)),
        _resident((None, D_MODEL, 2 * D_FF), lambda i: (layer, 0, 0)),
        _resident((None, D_FF, D_MODEL), lambda i: (layer, 0, 0)),
        pl.BlockSpec((1, D_MODEL), lambda i: (0, 0)),
    ]
    args = [x, gains, wi, wo, final_gain]
    out_specs = [pl.BlockSpec((tm, D_MODEL), lambda i: (i, 0))]
    out_shape = [jax.ShapeDtypeStruct((n, D_MODEL), F32)]
    aliases, dec, n_alias = {}, 0, 0
    if roll is not None:
        new_rows, caches, prev_outs, seq_base, dec = roll
        assert 2 * (caches[0].shape[1] // 2) == n // tm
        in_specs.append(pl.BlockSpec((SAMPLE_ROWS, QKV_DIM), lambda i: (seq_base + i // 2, 0)))
        args.append(new_rows)
        slab_specs = [pl.BlockSpec((None, None, None, GROUP_COLS, c.shape[-1]),
                                   lambda i: (layer, seq_base + i // 2, i % 2, 0, 0)) for c in caches]
        in_specs += slab_specs
        args += list(caches)
        if prev_outs is not None:
            n_alias = N_GROUPS
            for g, prev in enumerate(prev_outs):
                aliases[len(args)] = 1 + g
                in_specs.append(pl.BlockSpec(memory_space=pl.ANY))
                args.append(prev)
        out_specs += slab_specs
        out_shape += [jax.ShapeDtypeStruct(c.shape, F32) for c in caches]
    return pl.pallas_call(
        functools.partial(_ffn_kernel, final=final, dec=dec, n_alias=n_alias),
        grid=(n // tm,),
        in_specs=in_specs,
        out_specs=out_specs,
        out_shape=out_shape,
        input_output_aliases=aliases,
        scratch_shapes=[pltpu.VMEM((tm, D_MODEL), BF16), pltpu.VMEM((tm, D_MODEL), F32)],
        compiler_params=_params(1),
        name=name,
    )(*args)


def _qkv_kernel(x_ref, g_ref, w_ref, *refs):
    group_refs, tail_ref, u_s, z_s = refs[:N_GROUPS], refs[N_GROUPS], refs[N_GROUPS + 1], refs[N_GROUPS + 2]
    tm = x_ref.shape[0]
    x = x_ref[...]
    u_s[...] = ((x * _rms_inv(x)) * g_ref[...]).astype(BF16)
    for c in range(QKV_DIM // GROUP_COLS):
        kind, g = divmod(c, N_GROUPS)
        sl = slice(c * GROUP_COLS, (c + 1) * GROUP_COLS)
        z = jnp.dot(u_s[...], w_ref[:, sl], preferred_element_type=F32)
        if kind == 0:
            z = z * ATTN_SCALE
        tail_ref[:, sl] = z
        out = group_refs[g]
        dil = out.shape[0]
        osl = slice(kind * GROUP_COLS, (kind + 1) * GROUP_COLS)
        if dil == 1:
            out[0, :, osl] = z.astype(BF16)
        else:
            for t in range(LANE_TILES):
                z_s[t] = z[:, t * LANES:(t + 1) * LANES]
            for r in range(dil):
                for t in range(LANE_TILES):
                    lo = kind * GROUP_COLS + t * LANES
                    out[r, :, lo:lo + LANES] = z_s[t, pl.ds(r, tm // dil, stride=dil), :].astype(BF16)


def _qkv_call(x, gains, w, layer, tm, seq_rows, tail_rows, name):
    n = x.shape[0]
    tiles_per_seq = seq_rows // tm
    tail_tiles = tail_rows // tm
    n_seq = n // seq_rows

    def tail_map(i):
        j = jnp.maximum(i % tiles_per_seq - (tiles_per_seq - tail_tiles), 0)
        return ((i // tiles_per_seq) * tail_tiles + j, 0)

    group_specs = [pl.BlockSpec((None, dil, tm // dil, 3 * GROUP_COLS),
                                lambda i: (i // tiles_per_seq, 0, i % tiles_per_seq, 0))
                   for _, dil in GROUPS]
    group_shapes = [jax.ShapeDtypeStruct((n_seq, dil, seq_rows // dil, 3 * GROUP_COLS), BF16)
                    for _, dil in GROUPS]
    return pl.pallas_call(
        _qkv_kernel,
        grid=(n // tm,),
        in_specs=[
            pl.BlockSpec((tm, D_MODEL), lambda i: (i, 0)),
            pl.BlockSpec((None, 1, D_MODEL), lambda i: (layer, 0, 0)),
            _resident((None, D_MODEL, QKV_DIM), lambda i: (layer, 0, 0)),
        ],
        out_specs=group_specs + [pl.BlockSpec((tm, QKV_DIM), tail_map)],
        out_shape=group_shapes + [jax.ShapeDtypeStruct((n_seq * tail_rows, QKV_DIM), F32)],
        scratch_shapes=[pltpu.VMEM((tm, D_MODEL), BF16), pltpu.VMEM((LANE_TILES, tm, LANES), F32)],
        compiler_params=_params(1),
        name=name,
    )(x, gains, w)


CONV_CHUNK = 256
GATE_CHUNK = 512


def _mixer_kernel(*refs, sample, tiles_per_seq):
    x_ref, g_ref, wc_ref, wg_ref, bg_ref, cw_ref, wco_ref, wao_ref, wo_ref = refs[:9]
    refs = refs[9:]
    if sample:
        st_ref, refs = refs[0], refs[1:]
    attn_refs, refs = refs[:2 * N_GROUPS], refs[2 * N_GROUPS:]
    out_ref, p_ref, u_s, c_s, m_s = refs[:5]
    scratch = list(refs[5:])
    if not sample:
        halo_s = scratch.pop(0)
    tm = x_ref.shape[0]
    x = x_ref[...]
    u_s[...] = ((x * _rms_inv(x)) * g_ref[...]).astype(BF16)
    row = lax.broadcasted_iota(jnp.int32, (tm, CONV_CHUNK), 0)

    if not sample:
        @pl.when(pl.program_id(0) % tiles_per_seq == 0)
        def _():
            halo_s[...] = jnp.zeros_like(halo_s)

    for c in range(CONV_DIM // CONV_CHUNK):
        sl = slice(c * CONV_CHUNK, (c + 1) * CONV_CHUNK)
        u = u_s[...]
        cb = jnp.dot(u, wc_ref[:, sl], preferred_element_type=F32)
        cc = jnp.dot(u, wc_ref[:, CONV_DIM + c * CONV_CHUNK:CONV_DIM + (c + 1) * CONV_CHUNK],
                     preferred_element_type=F32)
        ch = jnp.dot(u, wc_ref[:, 2 * CONV_DIM + c * CONV_CHUNK:2 * CONV_DIM + (c + 1) * CONV_CHUNK],
                     preferred_element_type=F32)
        p = cc * ch
        if sample:
            p_ref[:, sl] = p
            p = jnp.where((row & (SAMPLE_ROWS - 1)) >= SAMPLE_ROWS - 2, st_ref[:, sl], p)
            r1 = pltpu.roll(p, 1, axis=0)
            r2 = pltpu.roll(p, 2, axis=0)
        else:
            r1 = pltpu.roll(p, 1, axis=0)
            r2 = pltpu.roll(p, 2, axis=0)
            h = halo_s[:, sl]
            r1 = jnp.where(row == 0, h[7:8], r1)
            r2 = jnp.where(row == 0, h[6:7], jnp.where(row == 1, h[7:8], r2))
            halo_s[:, sl] = p[tm - 8:tm]
            p_ref[:, sl] = p[tm - 8:tm]
        conv = cw_ref[0:1, sl] * r2 + cw_ref[1:2, sl] * r1 + cw_ref[2:3, sl] * p
        c_s[:, sl] = (cb * conv).astype(BF16)

    def natural(ref):
        dil = ref.shape[0]
        if dil == 1:
            return ref[0]
        s_ref = scratch.pop()
        for r in range(dil):
            for t in range(LANE_TILES):
                s_ref[t, pl.ds(r, tm // dil, stride=dil), :] = ref[r, :, t * LANES:(t + 1) * LANES]
        return jnp.concatenate([s_ref[t] for t in range(LANE_TILES)], axis=1)

    o = [natural(attn_refs[2 * g]) for g in range(N_GROUPS)]
    lse = [natural(attn_refs[2 * g + 1]) for g in range(N_GROUPS)]
    mx = jnp.maximum(jnp.maximum(lse[0], lse[1]), lse[2])
    e = [jnp.exp(l - mx) for l in lse]
    a = (e[0] * o[0] + e[1] * o[1] + e[2] * o[2]) / (e[0] + e[1] + e[2])
    aproj = jnp.dot(a.astype(BF16), wao_ref[...], preferred_element_type=F32)
    cproj = jnp.dot(c_s[...], wco_ref[...], preferred_element_type=F32)
    for c in range(D_MODEL // GATE_CHUNK):
        sl = slice(c * GATE_CHUNK, (c + 1) * GATE_CHUNK)
        za = jnp.dot(u_s[...], wg_ref[:, sl], preferred_element_type=F32) + bg_ref[:, sl]
        zc = (jnp.dot(u_s[...], wg_ref[:, D_MODEL + c * GATE_CHUNK:D_MODEL + (c + 1) * GATE_CHUNK],
                      preferred_element_type=F32)
              + bg_ref[:, D_MODEL + c * GATE_CHUNK:D_MODEL + (c + 1) * GATE_CHUNK])
        m = jax.nn.sigmoid(za) * aproj[:, sl] + jax.nn.sigmoid(zc) * cproj[:, sl]
        m_s[:, sl] = m.astype(BF16)
    out_ref[...] = x_ref[...] + jnp.dot(m_s[...], wo_ref[...], preferred_element_type=F32)


def _mixer_call(x, gains, wc, wg, bg, cw, wco, wao, wo, state, attn, layer, tm, seq_rows, name):
    n = x.shape[0]
    sample = state is not None
    tiles_per_seq = seq_rows // tm
    row_spec = pl.BlockSpec((tm, D_MODEL), lambda i: (i, 0))
    in_specs = [
        row_spec,
        pl.BlockSpec((None, 1, D_MODEL), lambda i: (layer, 0, 0)),
        _resident((None, D_MODEL, 3 * CONV_DIM), lambda i: (layer, 0, 0)),
        _resident((None, D_MODEL, 2 * D_MODEL), lambda i: (layer, 0, 0)),
        pl.BlockSpec((None, 1, 2 * D_MODEL), lambda i: (layer, 0, 0)),
        pl.BlockSpec((None, 3, CONV_DIM), lambda i: (layer, 0, 0)),
        _resident((None, CONV_DIM, D_MODEL), lambda i: (layer, 0, 0)),
        _resident((None, GROUP_COLS, D_MODEL), lambda i: (layer, 0, 0)),
        _resident((None, D_MODEL, D_MODEL), lambda i: (layer, 0, 0)),
    ]
    args = [x, gains, wc, wg, bg, cw, wco, wao, wo]
    scratch = [pltpu.VMEM((tm, D_MODEL), BF16), pltpu.VMEM((tm, CONV_DIM), BF16),
               pltpu.VMEM((tm, D_MODEL), BF16)]
    if sample:
        in_specs.append(row_spec)
        args.append(state)
        p_spec = row_spec
        p_shape = jax.ShapeDtypeStruct((n, CONV_DIM), F32)
    else:
        p_spec = pl.BlockSpec((8, CONV_DIM), lambda i: (i // tiles_per_seq, 0))
        p_shape = jax.ShapeDtypeStruct((n // seq_rows * 8, CONV_DIM), F32)
        scratch.append(pltpu.VMEM((8, CONV_DIM), F32))
    flat = [a for pair in attn for a in pair]
    in_specs += [pl.BlockSpec((None, a.shape[1], tm // a.shape[1], GROUP_COLS),
                              lambda i: (i // tiles_per_seq, 0, i % tiles_per_seq, 0))
                 for a in flat]
    scratch += [pltpu.VMEM((LANE_TILES, tm, LANES), F32)] * sum(a.shape[1] > 1 for a in flat)
    return pl.pallas_call(
        functools.partial(_mixer_kernel, sample=sample, tiles_per_seq=tiles_per_seq),
        grid=(n // tm,),
        in_specs=in_specs,
        out_specs=[row_spec, p_spec],
        out_shape=[jax.ShapeDtypeStruct((n, D_MODEL), F32), p_shape],
        scratch_shapes=scratch,
        compiler_params=_params(1),
        name=name,
    )(*args, *flat)


def _attn_prompt_kernel(q_ref, kp_ref, kc_ref, vp_ref, vc_ref, o_ref, l_ref):
    n = pl.program_id(2)
    row = lax.broadcasted_iota(jnp.int32, (2 * QBLOCK, 2 * QBLOCK), 0)
    col = lax.broadcasted_iota(jnp.int32, (2 * QBLOCK, 2 * QBLOCK), 1)
    qi = row & (QBLOCK - 1)
    cur_ok = (col >= QBLOCK) & (col - QBLOCK <= qi)
    valid_inner = cur_ok | ((col < QBLOCK) & (col >= qi))
    valid_first = cur_ok | ((col < QBLOCK) & (col >= qi + jnp.where(n > 0, 0, QBLOCK)))
    even = lax.broadcasted_iota(jnp.int32, (QBLOCK, 2 * HEAD_DIM), 1) < HEAD_DIM
    for i in range(ATTN_BLOCKS):
        rows = slice(i * QBLOCK, (i + 1) * QBLOCK)
        if i == 0:
            k = jnp.concatenate([kp_ref[...], kc_ref[rows]], axis=0)
            v = jnp.concatenate([vp_ref[...], vc_ref[rows]], axis=0)
            valid = valid_first
        else:
            k = kc_ref[(i - 1) * QBLOCK:(i + 1) * QBLOCK]
            v = vc_ref[(i - 1) * QBLOCK:(i + 1) * QBLOCK]
            valid = valid_inner
        for j in range(HEADS // 2):
            sl = slice(2 * HEAD_DIM * j, 2 * HEAD_DIM * (j + 1))
            qp = q_ref[rows, sl].astype(F32)
            lhs = jnp.concatenate([jnp.where(even, qp, 0.0), jnp.where(even, 0.0, qp)],
                                  axis=0).astype(BF16)
            s = lax.dot_general(lhs, k[:, sl], (((1,), (1,)), ((), ())),
                                preferred_element_type=F32)
            s = jnp.where(valid, s, NEG)
            m = jnp.max(s, axis=1, keepdims=True)
            p = jnp.exp(s - m)
            l = jnp.sum(p, axis=1, keepdims=True)
            pv = jnp.dot(p.astype(BF16), v[:, sl], preferred_element_type=F32)
            o = pv / l
            lse = jnp.broadcast_to(m + jnp.log(l), o.shape)
            o_ref[rows, sl] = jnp.where(even, o[:QBLOCK], o[QBLOCK:])
            l_ref[rows, sl] = jnp.where(even, lse[:QBLOCK], lse[QBLOCK:])


def _attn_prompt_call(qkv, name):
    batch, dil, length, _ = qkv.shape
    step_rows = ATTN_BLOCKS * QBLOCK
    nb = length // step_rows

    def spec(kind, prev):
        if prev:
            return pl.BlockSpec((None, None, QBLOCK, GROUP_COLS),
                                lambda b, r, n: (b, r, jnp.maximum(ATTN_BLOCKS * n - 1, 0), kind))
        return pl.BlockSpec((None, None, step_rows, GROUP_COLS), lambda b, r, n: (b, r, n, kind))

    out_spec = pl.BlockSpec((None, None, step_rows, GROUP_COLS), lambda b, r, n: (b, r, n, 0))
    out_shape = jax.ShapeDtypeStruct((batch, dil, length, GROUP_COLS), F32)
    return pl.pallas_call(
        _attn_prompt_kernel,
        grid=(batch, dil, nb),
        in_specs=[spec(0, False), spec(1, True), spec(1, False), spec(2, True), spec(2, False)],
        out_specs=[out_spec, out_spec],
        out_shape=[out_shape, out_shape],
        compiler_params=_params(3),
        name=name,
    )(qkv, qkv, qkv, qkv, qkv)


def _attn_sample_kernel(q_ref, kn_ref, vn_ref, c_ref, o_ref, l_ref, *, lb, dil, dec):
    q = q_ref[...]
    sub = lax.broadcasted_iota(jnp.int32, (SAMPLE_ROWS, GROUP_COLS), 0)
    lane = lax.broadcasted_iota(jnp.int32, (SAMPLE_ROWS, GROUP_COLS), 1)
    diag = sub == lane // HEAD_DIM
    qb = jnp.concatenate(
        [jnp.where(diag, jnp.broadcast_to(q[t:t + 1], diag.shape), 0.0) for t in range(dec)],
        axis=0).astype(BF16)
    rows = dec * HEADS
    kt = c_ref[0].astype(BF16)
    vt = c_ref[1].astype(BF16)
    s = jnp.dot(qb, kt, preferred_element_type=F32)
    t_of = lax.broadcasted_iota(jnp.int32, (rows, lb), 0) // HEADS
    diff = lax.broadcasted_iota(jnp.int32, (rows, lb), 1) - t_of
    valid = (diff >= 0) & ((diff & (dil - 1)) == 0)
    s = jnp.where(valid, s, NEG)
    m = jnp.max(s, axis=1, keepdims=True)

    qbf = qb.astype(F32)
    knr = kn_ref[...].astype(BF16).astype(F32)
    vnr = vn_ref[...].astype(BF16).astype(F32)
    t_col = lax.broadcasted_iota(jnp.int32, (rows, 1), 0) // HEADS
    s_new = []
    for t2 in range(dec):
        s2 = jnp.sum(qbf * knr[t2:t2 + 1], axis=1, keepdims=True)
        d2 = t_col - t2
        s2 = jnp.where((d2 >= 0) & ((d2 & (dil - 1)) == 0), s2, NEG)
        s_new.append(s2)
        m = jnp.maximum(m, s2)
    p = jnp.exp(s - m)
    l = jnp.sum(p, axis=1, keepdims=True)
    acc = lax.dot_general(p.astype(BF16), vt, (((1,), (1,)), ((), ())),
                          preferred_element_type=F32)
    for t2 in range(dec):
        pn = jnp.exp(s_new[t2] - m)
        l = l + pn
        acc = acc + pn * vnr[t2:t2 + 1]
    o = acc / l
    lse = jnp.broadcast_to(m + jnp.log(l), o.shape)
    o_rows, l_rows = [], []
    for t in range(dec):
        blk = slice(t * HEADS, (t + 1) * HEADS)
        o_rows.append(jnp.sum(jnp.where(diag, o[blk], 0.0), axis=0, keepdims=True))
        l_rows.append(jnp.sum(jnp.where(diag, lse[blk], 0.0), axis=0, keepdims=True))
    pad = jnp.zeros((SAMPLE_ROWS - dec, GROUP_COLS), F32)
    o_ref[...] = jnp.concatenate(o_rows + [pad], axis=0)
    l_ref[...] = jnp.concatenate(l_rows + [pad], axis=0)


def _attn_sample_call(qkv32, cache, layer, g, dil, dec, name):
    depth, n_seq, _, _, lb = cache.shape
    n = qkv32.shape[0]
    row = lambda off: pl.BlockSpec((SAMPLE_ROWS, GROUP_COLS), lambda s: (s, off + g))
    cache_spec = pl.BlockSpec((None, None, 2, GROUP_COLS, lb), lambda s: (layer, s, 0, 0, 0))
    out_row = pl.BlockSpec((SAMPLE_ROWS, GROUP_COLS), lambda s: (s, 0))
    return pl.pallas_call(
        functools.partial(_attn_sample_kernel, lb=lb, dil=dil, dec=dec),
        grid=(n_seq,),
        in_specs=[row(0), row(N_GROUPS), row(2 * N_GROUPS), cache_spec],
        out_specs=[out_row, out_row],
        out_shape=[jax.ShapeDtypeStruct((n, GROUP_COLS), F32),
                   jax.ShapeDtypeStruct((n, GROUP_COLS), F32)],
        compiler_params=_params(1),
        name=name,
    )(qkv32, qkv32, qkv32, cache)


def kernel(x_prompt, x_sample, cache_kv1, cache_kv2, cache_kv3, state_conv, ffn1_norm, ffn1_w_in,
           ffn1_w_out, mix_norm, w_in, b_gate, conv_w, w_attn_out, w_conv_out, w_out, ffn2_norm,
           ffn2_w_in, ffn2_w_out, final_norm):
    batch, seq, _ = x_prompt.shape
    n_seq, dec, _ = x_sample.shape
    depth = w_in.shape[0]
    caches = (cache_kv1, cache_kv2, cache_kv3)
    assert dec <= SAMPLE_ROWS - 2 and seq % (GROUPS[-1][1] * QBLOCK * ATTN_BLOCKS) == 0
    for cache, (win, dil) in zip(caches, GROUPS):
        assert cache.shape[2] == win and win == SPAN * dil and (dil == 1 or dec <= dil)

    f1_wi, f1_wo = ffn1_w_in.astype(BF16), ffn1_w_out.astype(BF16)
    f2_wi, f2_wo = ffn2_w_in.astype(BF16), ffn2_w_out.astype(BF16)
    w_qkv = w_in[:, :, :QKV_DIM].astype(BF16)
    w_conv = w_in[:, :, QKV_DIM:QKV_DIM + 3 * CONV_DIM].astype(BF16)
    w_gate = w_in[:, :, QKV_DIM + 3 * CONV_DIM:].astype(BF16)
    w_ao = w_attn_out.astype(BF16)
    w_co = w_conv_out.astype(BF16)
    w_o = w_out.astype(BF16)
    g1 = ffn1_norm.reshape(depth, 1, D_MODEL)
    gm = mix_norm.reshape(depth, 1, D_MODEL)
    g2 = ffn2_norm.reshape(depth, 1, D_MODEL)
    gf = final_norm.reshape(1, D_MODEL)
    bg = b_gate.reshape(depth, 1, 2 * D_MODEL)

    caches_t = [c.transpose(0, 1, 3, 4, 5, 2).reshape(depth, n_seq, 2, GROUP_COLS, c.shape[2])
                for c in caches]

    xp = x_prompt.reshape(batch * seq, D_MODEL)
    xs = jnp.pad(x_sample, ((0, 0), (0, SAMPLE_ROWS - dec), (0, 0))).reshape(n_seq * SAMPLE_ROWS, D_MODEL)
    ns = xs.shape[0]
    tail_rows = GROUPS[-1][0]

    kv_prompt = [[] for _ in GROUPS]
    conv_prompt, conv_sample = [], []
    s_tails = []
    for l in range(depth):
        last = l == depth - 1
        xs = _ffn_call(xs, g1, f1_wi, f1_wo, gf, l, ns, False, f"s_ffn1_{l}")[0]
        s_tail = _qkv_call(xs, gm, w_qkv, l, ns, ns, ns, f"s_qkv_{l}")[-1]
        s_tails.append(s_tail)
        st = jnp.pad(state_conv[l], ((0, 0), (SAMPLE_ROWS - 2, 0), (0, 0)))
        st = jnp.roll(st, -1, axis=0).reshape(ns, CONV_DIM)
        s_attn = []
        for g, (win, dil) in enumerate(GROUPS):
            o, lse = _attn_sample_call(s_tail, caches_t[g], l, g, dil, dec, f"s_attn{g}_{l}")
            s_attn.append((o.reshape(1, 1, ns, GROUP_COLS), lse.reshape(1, 1, ns, GROUP_COLS)))
        xs, s_p = _mixer_call(xs, gm, w_conv, w_gate, bg, conv_w, w_co, w_ao, w_o, st, s_attn, l,
                              ns, ns, f"s_mix_{l}")
        conv_sample.append(s_p.reshape(n_seq, SAMPLE_ROWS, CONV_DIM)[:, dec - 2:dec])
        xs = _ffn_call(xs, g2, f2_wi, f2_wo, gf, l, ns, last, f"s_ffn2_{l}")[0]

    new_caches = None
    for l in range(depth):
        last = l == depth - 1
        xp, *new_caches = _ffn_call(xp, g1, f1_wi, f1_wo, gf, l, ROW_TILE, False, f"p_ffn1_{l}",
                                    roll=(s_tails[l], caches_t, new_caches, 0, dec))
        *qkv_groups, p_tail = _qkv_call(xp, gm, w_qkv, l, ROW_TILE, seq, tail_rows, f"p_qkv_{l}")
        p_tail = p_tail.reshape(batch, tail_rows, QKV_DIM)
        p_attn = []
        for g, (win, dil) in enumerate(GROUPS):
            p_attn.append(_attn_prompt_call(qkv_groups[g], f"p_attn{g}_{l}"))
            keep = min(win, seq)
            kcol = ATTN_DIM + g * GROUP_COLS
            vcol = 2 * ATTN_DIM + g * GROUP_COLS
            k_t = p_tail[:, tail_rows - keep:, kcol:kcol + GROUP_COLS]
            v_t = p_tail[:, tail_rows - keep:, vcol:vcol + GROUP_COLS]
            kv_prompt[g].append(jnp.stack([k_t, v_t], axis=2).reshape(batch, keep, 2, HEADS, HEAD_DIM))
        xp, p_p = _mixer_call(xp, gm, w_conv, w_gate, bg, conv_w, w_co, w_ao, w_o, None, p_attn, l,
                              ROW_TILE, seq, f"p_mix_{l}")
        conv_prompt.append(p_p.reshape(batch, 8, CONV_DIM)[:, 6:8])
        xp, *new_caches = _ffn_call(xp, g2, f2_wi, f2_wo, gf, l, ROW_TILE, last, f"p_ffn2_{l}",
                                    roll=(s_tails[l], caches_t, new_caches, n_seq // 2, dec))

    y_prompt = xp.reshape(batch, seq, D_MODEL)
    y_sample = xs.reshape(n_seq, SAMPLE_ROWS, D_MODEL)[:, :dec]
    kv_sample = [c.reshape(depth, n_seq, 2, HEADS, HEAD_DIM, c.shape[-1]).transpose(0, 1, 5, 2, 3, 4)
                 for c in new_caches]
    return (y_prompt, y_sample,
            jnp.stack(kv_prompt[0]), jnp.stack(kv_prompt[1]), jnp.stack(kv_prompt[2]),
            jnp.stack(conv_prompt),
            kv_sample[0], kv_sample[1], kv_sample[2],
            jnp.stack(conv_sample))
```

```python
import functools

import jax
import jax.numpy as jnp
from jax import lax
from jax.experimental import pallas as pl
from jax.experimental.pallas import tpu as pltpu

F32 = jnp.float32
BF16 = jnp.bfloat16

D_MODEL = 1024
D_FF = 2816
HEAD_DIM = 64
HEADS = 8
GROUP_COLS = HEADS * HEAD_DIM
GROUPS = ((128, 1), (512, 4), (2048, 16))
N_GROUPS = len(GROUPS)
SPAN = 128
ATTN_DIM = N_GROUPS * GROUP_COLS
QKV_DIM = 3 * ATTN_DIM
CONV_DIM = D_MODEL
RMS_EPS = 1e-6
ATTN_SCALE = HEAD_DIM ** -0.5
NEG = -1e30

FF_CHUNK = 256
N_FF_CHUNKS = D_FF // FF_CHUNK
QBLOCK = 128
ATTN_BLOCKS = 4
LANES = 128
LANE_TILES = GROUP_COLS // LANES
SAMPLE_ROWS = 8
V7X_VMEM_LIMIT_BYTES = 56 * 1024 * 1024
ROW_TILE = 512


def _params(n_axes):
    return pltpu.CompilerParams(
        dimension_semantics=("arbitrary",) * n_axes,
        vmem_limit_bytes=V7X_VMEM_LIMIT_BYTES)


def _resident(block_shape, index_map):
    return pl.BlockSpec(block_shape, index_map, pipeline_mode=pl.Buffered(1))


def _rms_inv(x):
    return lax.rsqrt(jnp.mean(x * x, axis=-1, keepdims=True) + RMS_EPS)


def _roll_window(c_ref, new_rows, out_ref, dec):
    lb = c_ref.shape[1]
    rolled = pltpu.roll(c_ref[...], lb - dec, axis=1)
    out_ref[...] = rolled
    sub = lax.broadcasted_iota(jnp.int32, new_rows.shape, 0)
    shifted = jnp.where(sub >= SAMPLE_ROWS - dec,
                        pltpu.roll(new_rows, SAMPLE_ROWS - dec, axis=0), 0.0)
    padded = jnp.concatenate([jnp.zeros((LANES - SAMPLE_ROWS, GROUP_COLS), F32), shifted], axis=0)
    new_t = padded.T
    lane_t = lax.broadcasted_iota(jnp.int32, new_t.shape, 1)
    out_ref[:, lb - LANES:] = jnp.where(lane_t >= LANES - dec, new_t, rolled[:, lb - LANES:])


def _ffn_kernel(*refs, final, dec, n_alias):
    x_ref, g_ref, wi_ref, wo_ref, fg_ref = refs[:5]
    refs = refs[5:]
    if dec:
        new_refs, cache_refs = refs[:N_GROUPS], refs[N_GROUPS:2 * N_GROUPS]
        refs = refs[2 * N_GROUPS + n_alias:]
        o_ref, roll_refs, refs = refs[0], refs[1:1 + N_GROUPS], refs[1 + N_GROUPS:]
    else:
        o_ref, refs = refs[0], refs[1:]
    h_s, acc_s = refs
    if dec:
        is_v = (lax.broadcasted_iota(jnp.int32, (SAMPLE_ROWS, GROUP_COLS), 0) * 0
                + pl.program_id(0) % 2) == 1
        for g in range(N_GROUPS):
            new_rows = jnp.where(is_v, new_refs[g][:, 2 * GROUP_COLS:],
                                 new_refs[g][:, GROUP_COLS:2 * GROUP_COLS])
            _roll_window(cache_refs[g], new_rows, roll_refs[g], dec)
    x = x_ref[...]
    h_s[...] = ((x * _rms_inv(x)) * g_ref[...]).astype(BF16)
    for c in range(N_FF_CHUNKS):
        lo = c * FF_CHUNK
        gate = jnp.dot(h_s[...], wi_ref[:, lo:lo + FF_CHUNK], preferred_element_type=F32)
        up = jnp.dot(h_s[...], wi_ref[:, D_FF + lo:D_FF + lo + FF_CHUNK],
                     preferred_element_type=F32)
        act = ((gate * jax.nn.sigmoid(gate)) * up).astype(BF16)
        part = jnp.dot(act, wo_ref[lo:lo + FF_CHUNK, :], preferred_element_type=F32)
        if c == 0:
            acc_s[...] = part
        else:
            acc_s[...] += part
    y = x_ref[...] + 0.5 * acc_s[...]
    if final:
        y = (y * _rms_inv(y)) * fg_ref[...]
    o_ref[...] = y


def _ffn_call(x, gains, wi, wo, final_gain, layer, tm, final, name, roll=None):
    n = x.shape[0]
    in_specs = [
        pl.BlockSpec((tm, D_MODEL), lambda i: (i, 0)),
        pl.BlockSpec((None, 1, D_MODEL), lambda i: (layer, 0, 0)),
        _resident((None, D_MODEL, 2 * D_FF), lambda i: (layer, 0, 0)),
        _resident((None, D_FF, D_MODEL), lambda i: (layer, 0, 0)),
        pl.BlockSpec((1, D_MODEL), lambda i: (0, 0)),
    ]
    args = [x, gains, wi, wo, final_gain]
    out_specs = [pl.BlockSpec((tm, D_MODEL), lambda i: (i, 0))]
    out_shape = [jax.ShapeDtypeStruct((n, D_MODEL), F32)]
    aliases, dec, n_alias = {}, 0, 0
    if roll is not None:
        new_rows, caches, prev_outs, seq_base, dec = roll
        assert 2 * (caches[0].shape[1] // 2) == n // tm
        in_specs += [pl.BlockSpec((SAMPLE_ROWS, 3 * GROUP_COLS), lambda i: (seq_base + i // 2, 0))
                     for _ in new_rows]
        args += list(new_rows)
        slab_specs = [pl.BlockSpec((None, None, None, GROUP_COLS, c.shape[-1]),
                                   lambda i: (layer, seq_base + i // 2, i % 2, 0, 0)) for c in caches]
        in_specs += slab_specs
        args += list(caches)
        if prev_outs is not None:
            n_alias = N_GROUPS
            for g, prev in enumerate(prev_outs):
                aliases[len(args)] = 1 + g
                in_specs.append(pl.BlockSpec(memory_space=pl.ANY))
                args.append(prev)
        out_specs += slab_specs
        out_shape += [jax.ShapeDtypeStruct(c.shape, F32) for c in caches]
    return pl.pallas_call(
        functools.partial(_ffn_kernel, final=final, dec=dec, n_alias=n_alias),
        grid=(n // tm,),
        in_specs=in_specs,
        out_specs=out_specs,
        out_shape=out_shape,
        input_output_aliases=aliases,
        scratch_shapes=[pltpu.VMEM((tm, D_MODEL), BF16), pltpu.VMEM((tm, D_MODEL), F32)],
        compiler_params=_params(1),
        name=name,
    )(*args)


def _qkv_kernel(x_ref, g_ref, w_ref, *refs, dils):
    n_g = len(dils)
    bf_refs, f32_refs = refs[:n_g], refs[n_g:2 * n_g]
    u_s, slab_s = refs[2 * n_g], refs[2 * n_g + 1]
    perm_refs = list(refs[2 * n_g + 2:])
    tm = x_ref.shape[0]
    x = x_ref[...]
    u = (x * _rms_inv(x)) * g_ref[...]
    u_s[...] = u.astype(BF16)
    if perm_refs:
        for t in range(D_MODEL // LANES):
            slab_s[t] = u[:, t * LANES:(t + 1) * LANES]
    for g, dil in enumerate(dils):
        n = tm // dil
        lhs_ref = u_s
        if dil > 1:
            lhs_ref = perm_refs.pop()
            for r in range(dil):
                for t in range(D_MODEL // LANES):
                    lhs_ref[r * n:(r + 1) * n, t * LANES:(t + 1) * LANES] = (
                        slab_s[t, pl.ds(r, n, stride=dil), :].astype(BF16))
        for kind in range(3):
            col = (kind * n_g + g) * GROUP_COLS
            z = jnp.dot(lhs_ref[...], w_ref[:, col:col + GROUP_COLS], preferred_element_type=F32)
            if kind == 0:
                z = z * ATTN_SCALE
            osl = slice(kind * GROUP_COLS, (kind + 1) * GROUP_COLS)
            for r in range(dil):
                bf_refs[g][r, :, osl] = z[r * n:(r + 1) * n].astype(BF16)
                f32_refs[g][r, :, osl] = z[r * n:(r + 1) * n]


def _qkv_call(x, gains, w, layer, tm, seq_rows, tail_rows, dils, name):
    n = x.shape[0]
    tiles_per_seq = seq_rows // tm
    tail_tiles = tail_rows // tm
    n_seq = n // seq_rows

    def tail_map(i):
        return (i // tiles_per_seq, 0,
                jnp.maximum(i % tiles_per_seq - (tiles_per_seq - tail_tiles), 0), 0)

    block = lambda dil: (None, dil, tm // dil, 3 * GROUP_COLS)
    specs = [pl.BlockSpec(block(dil), lambda i: (i // tiles_per_seq, 0, i % tiles_per_seq, 0))
             for dil in dils]
    specs += [pl.BlockSpec(block(dil), tail_map) for dil in dils]
    shapes = [jax.ShapeDtypeStruct((n_seq, dil, seq_rows // dil, 3 * GROUP_COLS), BF16) for dil in dils]
    shapes += [jax.ShapeDtypeStruct((n_seq, dil, tail_rows // dil, 3 * GROUP_COLS), F32) for dil in dils]
    n_perm = sum(dil > 1 for dil in dils)
    return pl.pallas_call(
        functools.partial(_qkv_kernel, dils=dils),
        grid=(n // tm,),
        in_specs=[
            pl.BlockSpec((tm, D_MODEL), lambda i: (i, 0)),
            pl.BlockSpec((None, 1, D_MODEL), lambda i: (layer, 0, 0)),
            _resident((None, D_MODEL, QKV_DIM), lambda i: (layer, 0, 0)),
        ],
        out_specs=specs,
        out_shape=shapes,
        scratch_shapes=[pltpu.VMEM((tm, D_MODEL), BF16),
                        pltpu.VMEM((D_MODEL // LANES, tm, LANES), F32)]
                       + [pltpu.VMEM((tm, D_MODEL), BF16)] * n_perm,
        compiler_params=_params(1),
        name=name,
    )(x, gains, w)


CONV_CHUNK = 256
GATE_CHUNK = 512


def _mixer_kernel(*refs, sample, tiles_per_seq):
    x_ref, g_ref, wc_ref, wg_ref, bg_ref, cw_ref, wco_ref, wao_ref, wo_ref = refs[:9]
    refs = refs[9:]
    if sample:
        st_ref, refs = refs[0], refs[1:]
    attn_refs, refs = refs[:2 * N_GROUPS], refs[2 * N_GROUPS:]
    out_ref, p_ref, u_s, c_s, m_s = refs[:5]
    scratch = list(refs[5:])
    if not sample:
        halo_s = scratch.pop(0)
    tm = x_ref.shape[0]
    x = x_ref[...]
    u_s[...] = ((x * _rms_inv(x)) * g_ref[...]).astype(BF16)
    row = lax.broadcasted_iota(jnp.int32, (tm, CONV_CHUNK), 0)

    if not sample:
        @pl.when(pl.program_id(0) % tiles_per_seq == 0)
        def _():
            halo_s[...] = jnp.zeros_like(halo_s)

    for c in range(CONV_DIM // CONV_CHUNK):
        sl = slice(c * CONV_CHUNK, (c + 1) * CONV_CHUNK)
        u = u_s[...]
        cb = jnp.dot(u, wc_ref[:, sl], preferred_element_type=F32)
        cc = jnp.dot(u, wc_ref[:, CONV_DIM + c * CONV_CHUNK:CONV_DIM + (c + 1) * CONV_CHUNK],
                     preferred_element_type=F32)
        ch = jnp.dot(u, wc_ref[:, 2 * CONV_DIM + c * CONV_CHUNK:2 * CONV_DIM + (c + 1) * CONV_CHUNK],
                     preferred_element_type=F32)
        p = cc * ch
        if sample:
            p_ref[:, sl] = p
            p = jnp.where((row & (SAMPLE_ROWS - 1)) >= SAMPLE_ROWS - 2, st_ref[:, sl], p)
            r1 = pltpu.roll(p, 1, axis=0)
            r2 = pltpu.roll(p, 2, axis=0)
        else:
            r1 = pltpu.roll(p, 1, axis=0)
            r2 = pltpu.roll(p, 2, axis=0)
            h = halo_s[:, sl]
            r1 = jnp.where(row == 0, h[7:8], r1)
            r2 = jnp.where(row == 0, h[6:7], jnp.where(row == 1, h[7:8], r2))
            halo_s[:, sl] = p[tm - 8:tm]
            p_ref[:, sl] = p[tm - 8:tm]
        conv = cw_ref[0:1, sl] * r2 + cw_ref[1:2, sl] * r1 + cw_ref[2:3, sl] * p
        c_s[:, sl] = (cb * conv).astype(BF16)

    def natural(ref):
        dil = ref.shape[0]
        if dil == 1:
            return ref[0]
        s_ref = scratch.pop()
        for r in range(dil):
            for t in range(LANE_TILES):
                s_ref[t, pl.ds(r, tm // dil, stride=dil), :] = ref[r, :, t * LANES:(t + 1) * LANES]
        return jnp.concatenate([s_ref[t] for t in range(LANE_TILES)], axis=1)

    o = [natural(attn_refs[2 * g]) for g in range(N_GROUPS)]
    lse = [natural(attn_refs[2 * g + 1]) for g in range(N_GROUPS)]
    mx = jnp.maximum(jnp.maximum(lse[0], lse[1]), lse[2])
    e = [jnp.exp(l - mx) for l in lse]
    a = (e[0] * o[0] + e[1] * o[1] + e[2] * o[2]) / (e[0] + e[1] + e[2])
    aproj = jnp.dot(a.astype(BF16), wao_ref[...], preferred_element_type=F32)
    cproj = jnp.dot(c_s[...], wco_ref[...], preferred_element_type=F32)
    for c in range(D_MODEL // GATE_CHUNK):
        sl = slice(c * GATE_CHUNK, (c + 1) * GATE_CHUNK)
        za = jnp.dot(u_s[...], wg_ref[:, sl], preferred_element_type=F32) + bg_ref[:, sl]
        zc = (jnp.dot(u_s[...], wg_ref[:, D_MODEL + c * GATE_CHUNK:D_MODEL + (c + 1) * GATE_CHUNK],
                      preferred_element_type=F32)
              + bg_ref[:, D_MODEL + c * GATE_CHUNK:D_MODEL + (c + 1) * GATE_CHUNK])
        m = jax.nn.sigmoid(za) * aproj[:, sl] + jax.nn.sigmoid(zc) * cproj[:, sl]
        m_s[:, sl] = m.astype(BF16)
    out_ref[...] = x_ref[...] + jnp.dot(m_s[...], wo_ref[...], preferred_element_type=F32)


def _mixer_call(x, gains, wc, wg, bg, cw, wco, wao, wo, state, attn, layer, tm, seq_rows, name):
    n = x.shape[0]
    sample = state is not None
    tiles_per_seq = seq_rows // tm
    row_spec = pl.BlockSpec((tm, D_MODEL), lambda i: (i, 0))
    in_specs = [
        row_spec,
        pl.BlockSpec((None, 1, D_MODEL), lambda i: (layer, 0, 0)),
        _resident((None, D_MODEL, 3 * CONV_DIM), lambda i: (layer, 0, 0)),
        _resident((None, D_MODEL, 2 * D_MODEL), lambda i: (layer, 0, 0)),
        pl.BlockSpec((None, 1, 2 * D_MODEL), lambda i: (layer, 0, 0)),
        pl.BlockSpec((None, 3, CONV_DIM), lambda i: (layer, 0, 0)),
        _resident((None, CONV_DIM, D_MODEL), lambda i: (layer, 0, 0)),
        _resident((None, GROUP_COLS, D_MODEL), lambda i: (layer, 0, 0)),
        _resident((None, D_MODEL, D_MODEL), lambda i: (layer, 0, 0)),
    ]
    args = [x, gains, wc, wg, bg, cw, wco, wao, wo]
    scratch = [pltpu.VMEM((tm, D_MODEL), BF16), pltpu.VMEM((tm, CONV_DIM), BF16),
               pltpu.VMEM((tm, D_MODEL), BF16)]
    if sample:
        in_specs.append(row_spec)
        args.append(state)
        p_spec = row_spec
        p_shape = jax.ShapeDtypeStruct((n, CONV_DIM), F32)
    else:
        p_spec = pl.BlockSpec((8, CONV_DIM), lambda i: (i // tiles_per_seq, 0))
        p_shape = jax.ShapeDtypeStruct((n // seq_rows * 8, CONV_DIM), F32)
        scratch.append(pltpu.VMEM((8, CONV_DIM), F32))
    flat = [a for pair in attn for a in pair]
    in_specs += [pl.BlockSpec((None, a.shape[1], tm // a.shape[1], GROUP_COLS),
                              lambda i: (i // tiles_per_seq, 0, i % tiles_per_seq, 0))
                 for a in flat]
    scratch += [pltpu.VMEM((LANE_TILES, tm, LANES), F32)] * sum(a.shape[1] > 1 for a in flat)
    return pl.pallas_call(
        functools.partial(_mixer_kernel, sample=sample, tiles_per_seq=tiles_per_seq),
        grid=(n // tm,),
        in_specs=in_specs,
        out_specs=[row_spec, p_spec],
        out_shape=[jax.ShapeDtypeStruct((n, D_MODEL), F32), p_shape],
        scratch_shapes=scratch,
        compiler_params=_params(1),
        name=name,
    )(*args, *flat)


def _attn_prompt_kernel(q_ref, kp_ref, kc_ref, vp_ref, vc_ref, o_ref, l_ref):
    n = pl.program_id(2)
    row = lax.broadcasted_iota(jnp.int32, (2 * QBLOCK, 2 * QBLOCK), 0)
    col = lax.broadcasted_iota(jnp.int32, (2 * QBLOCK, 2 * QBLOCK), 1)
    qi = row & (QBLOCK - 1)
    cur_ok = (col >= QBLOCK) & (col - QBLOCK <= qi)
    valid_inner = cur_ok | ((col < QBLOCK) & (col >= qi))
    valid_first = cur_ok | ((col < QBLOCK) & (col >= qi + jnp.where(n > 0, 0, QBLOCK)))
    even = lax.broadcasted_iota(jnp.int32, (QBLOCK, 2 * HEAD_DIM), 1) < HEAD_DIM
    for i in range(ATTN_BLOCKS):
        rows = slice(i * QBLOCK, (i + 1) * QBLOCK)
        if i == 0:
            k = jnp.concatenate([kp_ref[...], kc_ref[rows]], axis=0)
            v = jnp.concatenate([vp_ref[...], vc_ref[rows]], axis=0)
            valid = valid_first
        else:
            k = kc_ref[(i - 1) * QBLOCK:(i + 1) * QBLOCK]
            v = vc_ref[(i - 1) * QBLOCK:(i + 1) * QBLOCK]
            valid = valid_inner
        for j in range(HEADS // 2):
            sl = slice(2 * HEAD_DIM * j, 2 * HEAD_DIM * (j + 1))
            qp = q_ref[rows, sl].astype(F32)
            lhs = jnp.concatenate([jnp.where(even, qp, 0.0), jnp.where(even, 0.0, qp)],
                                  axis=0).astype(BF16)
            s = lax.dot_general(lhs, k[:, sl], (((1,), (1,)), ((), ())),
                                preferred_element_type=F32)
            s = jnp.where(valid, s, NEG)
            m = jnp.max(s, axis=1, keepdims=True)
            p = jnp.exp(s - m)
            l = jnp.sum(p, axis=1, keepdims=True)
            pv = jnp.dot(p.astype(BF16), v[:, sl], preferred_element_type=F32)
            o = pv / l
            lse = jnp.broadcast_to(m + jnp.log(l), o.shape)
            o_ref[rows, sl] = jnp.where(even, o[:QBLOCK], o[QBLOCK:])
            l_ref[rows, sl] = jnp.where(even, lse[:QBLOCK], lse[QBLOCK:])


def _attn_prompt_call(qkv, name):
    batch, dil, length, _ = qkv.shape
    step_rows = ATTN_BLOCKS * QBLOCK
    nb = length // step_rows

    def spec(kind, prev):
        if prev:
            return pl.BlockSpec((None, None, QBLOCK, GROUP_COLS),
                                lambda b, r, n: (b, r, jnp.maximum(ATTN_BLOCKS * n - 1, 0), kind))
        return pl.BlockSpec((None, None, step_rows, GROUP_COLS), lambda b, r, n: (b, r, n, kind))

    out_spec = pl.BlockSpec((None, None, step_rows, GROUP_COLS), lambda b, r, n: (b, r, n, 0))
    out_shape = jax.ShapeDtypeStruct((batch, dil, length, GROUP_COLS), F32)
    return pl.pallas_call(
        _attn_prompt_kernel,
        grid=(batch, dil, nb),
        in_specs=[spec(0, False), spec(1, True), spec(1, False), spec(2, True), spec(2, False)],
        out_specs=[out_spec, out_spec],
        out_shape=[out_shape, out_shape],
        compiler_params=_params(3),
        name=name,
    )(qkv, qkv, qkv, qkv, qkv)


def _attn_sample_kernel(*refs, dils, dec):
    n_g = len(dils)
    for g, dil in enumerate(dils):
        q_ref, kn_ref, vn_ref = refs[3 * g:3 * g + 3]
        c_ref = refs[3 * n_g + g]
        o_ref, l_ref = refs[4 * n_g + 2 * g:4 * n_g + 2 * g + 2]
        _attn_sample_group(q_ref, kn_ref, vn_ref, c_ref, o_ref, l_ref, dil, dec)


def _attn_sample_group(q_ref, kn_ref, vn_ref, c_ref, o_ref, l_ref, dil, dec):
    lb = c_ref.shape[-1]
    q = q_ref[...]
    sub = lax.broadcasted_iota(jnp.int32, (SAMPLE_ROWS, GROUP_COLS), 0)
    lane = lax.broadcasted_iota(jnp.int32, (SAMPLE_ROWS, GROUP_COLS), 1)
    diag = sub == lane // HEAD_DIM
    qb = jnp.concatenate(
        [jnp.where(diag, jnp.broadcast_to(q[t:t + 1], diag.shape), 0.0) for t in range(dec)],
        axis=0).astype(BF16)
    rows = dec * HEADS
    kt = c_ref[0].astype(BF16)
    vt = c_ref[1].astype(BF16)
    s = jnp.dot(qb, kt, preferred_element_type=F32)
    t_of = lax.broadcasted_iota(jnp.int32, (rows, lb), 0) // HEADS
    diff = lax.broadcasted_iota(jnp.int32, (rows, lb), 1) - t_of
    valid = (diff >= 0) & ((diff & (dil - 1)) == 0)
    s = jnp.where(valid, s, NEG)
    m = jnp.max(s, axis=1, keepdims=True)

    qbf = qb.astype(F32)
    knr = kn_ref[...].astype(BF16).astype(F32)
    vnr = vn_ref[...].astype(BF16).astype(F32)
    t_col = lax.broadcasted_iota(jnp.int32, (rows, 1), 0) // HEADS
    s_new = []
    for t2 in range(dec):
        s2 = jnp.sum(qbf * knr[t2:t2 + 1], axis=1, keepdims=True)
        d2 = t_col - t2
        s2 = jnp.where((d2 >= 0) & ((d2 & (dil - 1)) == 0), s2, NEG)
        s_new.append(s2)
        m = jnp.maximum(m, s2)
    p = jnp.exp(s - m)
    l = jnp.sum(p, axis=1, keepdims=True)
    acc = lax.dot_general(p.astype(BF16), vt, (((1,), (1,)), ((), ())),
                          preferred_element_type=F32)
    for t2 in range(dec):
        pn = jnp.exp(s_new[t2] - m)
        l = l + pn
        acc = acc + pn * vnr[t2:t2 + 1]
    o = acc / l
    lse = jnp.broadcast_to(m + jnp.log(l), o.shape)
    o_rows, l_rows = [], []
    for t in range(dec):
        blk = slice(t * HEADS, (t + 1) * HEADS)
        o_rows.append(jnp.sum(jnp.where(diag, o[blk], 0.0), axis=0, keepdims=True))
        l_rows.append(jnp.sum(jnp.where(diag, lse[blk], 0.0), axis=0, keepdims=True))
    pad = jnp.zeros((SAMPLE_ROWS - dec, GROUP_COLS), F32)
    o_ref[...] = jnp.concatenate(o_rows + [pad], axis=0)
    l_ref[...] = jnp.concatenate(l_rows + [pad], axis=0)


def _attn_sample_call(qkv32, caches, layer, dils, dec, name):
    n_seq = caches[0].shape[1]
    n = qkv32[0].shape[0]
    in_specs, args = [], []
    for a in qkv32:
        for kind in range(3):
            in_specs.append(pl.BlockSpec((SAMPLE_ROWS, GROUP_COLS), lambda s, kind=kind: (s, kind)))
            args.append(a)
    for c in caches:
        in_specs.append(pl.BlockSpec((None, None, 2, GROUP_COLS, c.shape[-1]),
                                     lambda s: (layer, s, 0, 0, 0)))
        args.append(c)
    out_row = pl.BlockSpec((SAMPLE_ROWS, GROUP_COLS), lambda s: (s, 0))
    outs = pl.pallas_call(
        functools.partial(_attn_sample_kernel, dils=dils, dec=dec),
        grid=(n_seq,),
        in_specs=in_specs,
        out_specs=[out_row] * (2 * len(dils)),
        out_shape=[jax.ShapeDtypeStruct((n, GROUP_COLS), F32)] * (2 * len(dils)),
        compiler_params=_params(1),
        name=name,
    )(*args)
    return [(outs[2 * g], outs[2 * g + 1]) for g in range(len(dils))]


def kernel(x_prompt, x_sample, cache_kv1, cache_kv2, cache_kv3, state_conv, ffn1_norm, ffn1_w_in,
           ffn1_w_out, mix_norm, w_in, b_gate, conv_w, w_attn_out, w_conv_out, w_out, ffn2_norm,
           ffn2_w_in, ffn2_w_out, final_norm):
    batch, seq, _ = x_prompt.shape
    n_seq, dec, _ = x_sample.shape
    depth = w_in.shape[0]
    caches = (cache_kv1, cache_kv2, cache_kv3)
    assert dec <= SAMPLE_ROWS - 2 and seq % (GROUPS[-1][1] * QBLOCK * ATTN_BLOCKS) == 0
    for cache, (win, dil) in zip(caches, GROUPS):
        assert cache.shape[2] == win and win == SPAN * dil and (dil == 1 or dec <= dil)

    f1_wi, f1_wo = ffn1_w_in.astype(BF16), ffn1_w_out.astype(BF16)
    f2_wi, f2_wo = ffn2_w_in.astype(BF16), ffn2_w_out.astype(BF16)
    w_qkv = w_in[:, :, :QKV_DIM].astype(BF16)
    w_conv = w_in[:, :, QKV_DIM:QKV_DIM + 3 * CONV_DIM].astype(BF16)
    w_gate = w_in[:, :, QKV_DIM + 3 * CONV_DIM:].astype(BF16)
    w_ao = w_attn_out.astype(BF16)
    w_co = w_conv_out.astype(BF16)
    w_o = w_out.astype(BF16)
    g1 = ffn1_norm.reshape(depth, 1, D_MODEL)
    gm = mix_norm.reshape(depth, 1, D_MODEL)
    g2 = ffn2_norm.reshape(depth, 1, D_MODEL)
    gf = final_norm.reshape(1, D_MODEL)
    bg = b_gate.reshape(depth, 1, 2 * D_MODEL)

    caches_t = [c.transpose(0, 1, 3, 4, 5, 2).reshape(depth, n_seq, 2, GROUP_COLS, c.shape[2])
                for c in caches]

    xp = x_prompt.reshape(batch * seq, D_MODEL)
    xs = jnp.pad(x_sample, ((0, 0), (0, SAMPLE_ROWS - dec), (0, 0))).reshape(n_seq * SAMPLE_ROWS, D_MODEL)
    ns = xs.shape[0]
    tail_rows = GROUPS[-1][0]
    dils = tuple(dil for _, dil in GROUPS)

    kv_prompt = [[] for _ in GROUPS]
    conv_prompt, conv_sample = [], []
    s_tails = []
    for l in range(depth):
        last = l == depth - 1
        xs = _ffn_call(xs, g1, f1_wi, f1_wo, gf, l, ns, False, f"s_ffn1_{l}")[0]
        s_qkv = _qkv_call(xs, gm, w_qkv, l, ns, ns, ns, (1,) * N_GROUPS, f"s_qkv_{l}")[N_GROUPS:]
        s_tail = [a.reshape(ns, 3 * GROUP_COLS) for a in s_qkv]
        s_tails.append(s_tail)
        st = jnp.pad(state_conv[l], ((0, 0), (SAMPLE_ROWS - 2, 0), (0, 0)))
        st = jnp.roll(st, -1, axis=0).reshape(ns, CONV_DIM)
        s_attn = [(o.reshape(1, 1, ns, GROUP_COLS), lse.reshape(1, 1, ns, GROUP_COLS))
                  for o, lse in _attn_sample_call(s_tail, caches_t, l, dils, dec, f"s_attn_{l}")]
        xs, s_p = _mixer_call(xs, gm, w_conv, w_gate, bg, conv_w, w_co, w_ao, w_o, st, s_attn, l,
                              ns, ns, f"s_mix_{l}")
        conv_sample.append(s_p.reshape(n_seq, SAMPLE_ROWS, CONV_DIM)[:, dec - 2:dec])
        xs = _ffn_call(xs, g2, f2_wi, f2_wo, gf, l, ns, last, f"s_ffn2_{l}")[0]

    new_caches = None
    for l in range(depth):
        last = l == depth - 1
        xp, *new_caches = _ffn_call(xp, g1, f1_wi, f1_wo, gf, l, ROW_TILE, False, f"p_ffn1_{l}",
                                    roll=(s_tails[l], caches_t, new_caches, 0, dec))
        p_qkv = _qkv_call(xp, gm, w_qkv, l, ROW_TILE, seq, tail_rows, dils, f"p_qkv_{l}")
        p_attn = []
        for g, (win, dil) in enumerate(GROUPS):
            p_attn.append(_attn_prompt_call(p_qkv[g], f"p_attn{g}_{l}"))
            keep = min(win, seq)
            tail = p_qkv[N_GROUPS + g].transpose(0, 2, 1, 3).reshape(batch, tail_rows, 3 * GROUP_COLS)
            k_t = tail[:, tail_rows - keep:, GROUP_COLS:2 * GROUP_COLS]
            v_t = tail[:, tail_rows - keep:, 2 * GROUP_COLS:]
            kv_prompt[g].append(jnp.stack([k_t, v_t], axis=2).reshape(batch, keep, 2, HEADS, HEAD_DIM))
        xp, p_p = _mixer_call(xp, gm, w_conv, w_gate, bg, conv_w, w_co, w_ao, w_o, None, p_attn, l,
                              ROW_TILE, seq, f"p_mix_{l}")
        conv_prompt.append(p_p.reshape(batch, 8, CONV_DIM)[:, 6:8])
        xp, *new_caches = _ffn_call(xp, g2, f2_wi, f2_wo, gf, l, ROW_TILE, last, f"p_ffn2_{l}",
                                    roll=(s_tails[l], caches_t, new_caches, n_seq // 2, dec))

    y_prompt = xp.reshape(batch, seq, D_MODEL)
    y_sample = xs.reshape(n_seq, SAMPLE_ROWS, D_MODEL)[:, :dec]
    kv_sample = [c.reshape(depth, n_seq, 2, HEADS, HEAD_DIM, c.shape[-1]).transpose(0, 1, 5, 2, 3, 4)
                 for c in new_caches]
    return (y_prompt, y_sample,
            jnp.stack(kv_prompt[0]), jnp.stack(kv_prompt[1]), jnp.stack(kv_prompt[2]),
            jnp.stack(conv_prompt),
            kv_sample[0], kv_sample[1], kv_sample[2],
            jnp.stack(conv_sample))
```

```python
import functools

import jax
import jax.numpy as jnp
from jax import lax
from jax.experimental import pallas as pl
from jax.experimental.pallas import tpu as pltpu

F32 = jnp.float32
BF16 = jnp.bfloat16

D_MODEL = 1024
D_FF = 2816
HEAD_DIM = 64
HEADS = 8
GROUP_COLS = HEADS * HEAD_DIM
GROUPS = ((128, 1), (512, 4), (2048, 16))
N_GROUPS = len(GROUPS)
SPAN = 128
ATTN_DIM = N_GROUPS * GROUP_COLS
QKV_DIM = 3 * ATTN_DIM
CONV_DIM = D_MODEL
RMS_EPS = 1e-6
ATTN_SCALE = HEAD_DIM ** -0.5
LOG2_E = 1.4426950408889634
LN_2 = 0.6931471805599453
NEG = -1e30

FF_CHUNK = 256
N_FF_CHUNKS = D_FF // FF_CHUNK
QBLOCK = 128
ATTN_BLOCKS = 4
LANES = 128
LANE_TILES = GROUP_COLS // LANES
SAMPLE_ROWS = 8
V7X_VMEM_LIMIT_BYTES = 56 * 1024 * 1024
ROW_TILE = 512


def _params(n_axes):
    return pltpu.CompilerParams(
        dimension_semantics=("arbitrary",) * n_axes,
        vmem_limit_bytes=V7X_VMEM_LIMIT_BYTES)


def _resident(block_shape, index_map):
    return pl.BlockSpec(block_shape, index_map, pipeline_mode=pl.Buffered(1))


def _rms_inv(x):
    return lax.rsqrt(jnp.mean(x * x, axis=-1, keepdims=True) + RMS_EPS)


def _cast_split_kernel(w_ref, *out_refs):
    off = 0
    for out in out_refs:
        width = out.shape[-1]
        out[...] = w_ref[:, off:off + width].astype(BF16)
        off += width


def _cast_split_call(w, widths, rows, name):
    depth, r, c = w.shape
    assert sum(widths) == c and r % rows == 0
    return pl.pallas_call(
        _cast_split_kernel,
        grid=(depth, r // rows),
        in_specs=[pl.BlockSpec((None, rows, c), lambda l, i: (l, i, 0))],
        out_specs=[pl.BlockSpec((None, rows, wd), lambda l, i: (l, i, 0)) for wd in widths],
        out_shape=[jax.ShapeDtypeStruct((depth, r, wd), BF16) for wd in widths],
        compiler_params=_params(2),
        name=name,
    )(w)


def _roll_window(c_ref, new_rows, out_ref, dec):
    lb = c_ref.shape[1]
    rolled = pltpu.roll(c_ref[...], lb - dec, axis=1)
    out_ref[...] = rolled
    sub = lax.broadcasted_iota(jnp.int32, new_rows.shape, 0)
    shifted = jnp.where(sub >= SAMPLE_ROWS - dec,
                        pltpu.roll(new_rows, SAMPLE_ROWS - dec, axis=0), 0.0)
    padded = jnp.concatenate([jnp.zeros((LANES - SAMPLE_ROWS, GROUP_COLS), F32), shifted], axis=0)
    new_t = padded.T
    lane_t = lax.broadcasted_iota(jnp.int32, new_t.shape, 1)
    out_ref[:, lb - LANES:] = jnp.where(lane_t >= LANES - dec, new_t, rolled[:, lb - LANES:])


def _ffn_kernel(*refs, final, dec, n_alias):
    x_ref, g_ref, wi_ref, wo_ref, fg_ref = refs[:5]
    refs = refs[5:]
    if dec:
        new_refs, cache_refs = refs[:N_GROUPS], refs[N_GROUPS:2 * N_GROUPS]
        refs = refs[2 * N_GROUPS + n_alias:]
        o_ref, roll_refs, refs = refs[0], refs[1:1 + N_GROUPS], refs[1 + N_GROUPS:]
    else:
        o_ref, refs = refs[0], refs[1:]
    h_s, acc_s = refs
    if dec:
        is_v = (lax.broadcasted_iota(jnp.int32, (SAMPLE_ROWS, GROUP_COLS), 0) * 0
                + pl.program_id(0) % 2) == 1
        for g in range(N_GROUPS):
            new_rows = jnp.where(is_v, new_refs[g][:, 2 * GROUP_COLS:],
                                 new_refs[g][:, GROUP_COLS:2 * GROUP_COLS])
            _roll_window(cache_refs[g], new_rows, roll_refs[g], dec)
    x = x_ref[...]
    h_s[...] = ((x * _rms_inv(x)) * g_ref[...]).astype(BF16)
    for c in range(N_FF_CHUNKS):
        lo = c * FF_CHUNK
        gate = jnp.dot(h_s[...], wi_ref[:, lo:lo + FF_CHUNK], preferred_element_type=F32)
        up = jnp.dot(h_s[...], wi_ref[:, D_FF + lo:D_FF + lo + FF_CHUNK],
                     preferred_element_type=F32)
        act = ((gate * jax.nn.sigmoid(gate)) * up).astype(BF16)
        part = jnp.dot(act, wo_ref[lo:lo + FF_CHUNK, :], preferred_element_type=F32)
        if c == 0:
            acc_s[...] = part
        else:
            acc_s[...] += part
    y = x_ref[...] + 0.5 * acc_s[...]
    if final:
        y = (y * _rms_inv(y)) * fg_ref[...]
    o_ref[...] = y


def _ffn_call(x, gains, wi, wo, final_gain, layer, tm, final, name, roll=None):
    n = x.shape[0]
    in_specs = [
        pl.BlockSpec((tm, D_MODEL), lambda i: (i, 0)),
        pl.BlockSpec((None, 1, D_MODEL), lambda i: (layer, 0, 0)),
        _resident((None, D_MODEL, 2 * D_FF), lambda i: (layer, 0, 0)),
        _resident((None, D_FF, D_MODEL), lambda i: (layer, 0, 0)),
        pl.BlockSpec((1, D_MODEL), lambda i: (0, 0)),
    ]
    args = [x, gains, wi, wo, final_gain]
    out_specs = [pl.BlockSpec((tm, D_MODEL), lambda i: (i, 0))]
    out_shape = [jax.ShapeDtypeStruct((n, D_MODEL), F32)]
    aliases, dec, n_alias = {}, 0, 0
    if roll is not None:
        new_rows, caches, prev_outs, seq_base, dec = roll
        assert 2 * (caches[0].shape[1] // 2) == n // tm
        in_specs += [pl.BlockSpec((SAMPLE_ROWS, 3 * GROUP_COLS), lambda i: (seq_base + i // 2, 0))
                     for _ in new_rows]
        args += list(new_rows)
        slab_specs = [pl.BlockSpec((None, None, None, GROUP_COLS, c.shape[-1]),
                                   lambda i: (layer, seq_base + i // 2, i % 2, 0, 0)) for c in caches]
        in_specs += slab_specs
        args += list(caches)
        if prev_outs is not None:
            n_alias = N_GROUPS
            for g, prev in enumerate(prev_outs):
                aliases[len(args)] = 1 + g
                in_specs.append(pl.BlockSpec(memory_space=pl.ANY))
                args.append(prev)
        out_specs += slab_specs
        out_shape += [jax.ShapeDtypeStruct(c.shape, F32) for c in caches]
    return pl.pallas_call(
        functools.partial(_ffn_kernel, final=final, dec=dec, n_alias=n_alias),
        grid=(n // tm,),
        in_specs=in_specs,
        out_specs=out_specs,
        out_shape=out_shape,
        input_output_aliases=aliases,
        scratch_shapes=[pltpu.VMEM((tm, D_MODEL), BF16), pltpu.VMEM((tm, D_MODEL), F32)],
        compiler_params=_params(1),
        name=name,
    )(*args)


def _qkv_kernel(x_ref, g_ref, w_ref, *refs, dils):
    n_g = len(dils)
    bf_refs, f32_refs = refs[:n_g], refs[n_g:2 * n_g]
    u_s, slab_s = refs[2 * n_g], refs[2 * n_g + 1]
    perm_refs = list(refs[2 * n_g + 2:])
    tm = x_ref.shape[0]
    x = x_ref[...]
    u = (x * _rms_inv(x)) * g_ref[...]
    u_s[...] = u.astype(BF16)
    if perm_refs:
        for t in range(D_MODEL // LANES):
            slab_s[t] = u[:, t * LANES:(t + 1) * LANES]
    for g, dil in enumerate(dils):
        n = tm // dil
        lhs_ref = u_s
        if dil > 1:
            lhs_ref = perm_refs.pop()
            for r in range(dil):
                for t in range(D_MODEL // LANES):
                    lhs_ref[r * n:(r + 1) * n, t * LANES:(t + 1) * LANES] = (
                        slab_s[t, pl.ds(r, n, stride=dil), :].astype(BF16))
        even = (lax.broadcasted_iota(jnp.int32, (n, GROUP_COLS), 1) & (2 * HEAD_DIM - 1)) < HEAD_DIM
        for kind in range(3):
            col = (kind * n_g + g) * GROUP_COLS
            z = jnp.dot(lhs_ref[...], w_ref[:, col:col + GROUP_COLS], preferred_element_type=F32)
            if kind == 0:
                z = z * ATTN_SCALE
            osl = slice(kind * GROUP_COLS, (kind + 1) * GROUP_COLS)
            for r in range(dil):
                zr = z[r * n:(r + 1) * n]
                f32_refs[g][r, :, osl] = zr
                if kind == 0:
                    zq = zr * LOG2_E
                    bf_refs[g][r, :, :GROUP_COLS] = jnp.where(even, zq, 0.0).astype(BF16)
                    bf_refs[g][r, :, GROUP_COLS:2 * GROUP_COLS] = jnp.where(even, 0.0, zq).astype(BF16)
                else:
                    bf_refs[g][r, :, (kind + 1) * GROUP_COLS:(kind + 2) * GROUP_COLS] = zr.astype(BF16)


def _qkv_call(x, gains, w, layer, tm, seq_rows, tail_rows, dils, name):
    n = x.shape[0]
    tiles_per_seq = seq_rows // tm
    tail_tiles = tail_rows // tm
    n_seq = n // seq_rows

    def tail_map(i):
        return (i // tiles_per_seq, 0,
                jnp.maximum(i % tiles_per_seq - (tiles_per_seq - tail_tiles), 0), 0)

    block = lambda dil, parts=3: (None, dil, tm // dil, parts * GROUP_COLS)
    specs = [pl.BlockSpec(block(dil, 4), lambda i: (i // tiles_per_seq, 0, i % tiles_per_seq, 0))
             for dil in dils]
    specs += [pl.BlockSpec(block(dil), tail_map) for dil in dils]
    shapes = [jax.ShapeDtypeStruct((n_seq, dil, seq_rows // dil, 4 * GROUP_COLS), BF16) for dil in dils]
    shapes += [jax.ShapeDtypeStruct((n_seq, dil, tail_rows // dil, 3 * GROUP_COLS), F32) for dil in dils]
    n_perm = sum(dil > 1 for dil in dils)
    return pl.pallas_call(
        functools.partial(_qkv_kernel, dils=dils),
        grid=(n // tm,),
        in_specs=[
            pl.BlockSpec((tm, D_MODEL), lambda i: (i, 0)),
            pl.BlockSpec((None, 1, D_MODEL), lambda i: (layer, 0, 0)),
            _resident((None, D_MODEL, QKV_DIM), lambda i: (layer, 0, 0)),
        ],
        out_specs=specs,
        out_shape=shapes,
        scratch_shapes=[pltpu.VMEM((tm, D_MODEL), BF16),
                        pltpu.VMEM((D_MODEL // LANES, tm, LANES), F32)]
                       + [pltpu.VMEM((tm, D_MODEL), BF16)] * n_perm,
        compiler_params=_params(1),
        name=name,
    )(x, gains, w)


CONV_CHUNK = 256
GATE_CHUNK = 512


def _mixer_kernel(*refs, sample, tiles_per_seq):
    x_ref, g_ref, wc_ref, wg_ref, bg_ref, cw_ref, wco_ref, wao_ref, wo_ref = refs[:9]
    refs = refs[9:]
    if sample:
        st_ref, refs = refs[0], refs[1:]
    attn_refs, refs = refs[:2 * N_GROUPS], refs[2 * N_GROUPS:]
    out_ref, p_ref, u_s, c_s, m_s = refs[:5]
    scratch = list(refs[5:])
    if not sample:
        halo_s = scratch.pop(0)
    tm = x_ref.shape[0]
    x = x_ref[...]
    u_s[...] = ((x * _rms_inv(x)) * g_ref[...]).astype(BF16)
    row = lax.broadcasted_iota(jnp.int32, (tm, CONV_CHUNK), 0)

    if not sample:
        @pl.when(pl.program_id(0) % tiles_per_seq == 0)
        def _():
            halo_s[...] = jnp.zeros_like(halo_s)

    for c in range(CONV_DIM // CONV_CHUNK):
        sl = slice(c * CONV_CHUNK, (c + 1) * CONV_CHUNK)
        u = u_s[...]
        cb = jnp.dot(u, wc_ref[:, sl], preferred_element_type=F32)
        cc = jnp.dot(u, wc_ref[:, CONV_DIM + c * CONV_CHUNK:CONV_DIM + (c + 1) * CONV_CHUNK],
                     preferred_element_type=F32)
        ch = jnp.dot(u, wc_ref[:, 2 * CONV_DIM + c * CONV_CHUNK:2 * CONV_DIM + (c + 1) * CONV_CHUNK],
                     preferred_element_type=F32)
        p = cc * ch
        if sample:
            p_ref[:, sl] = p
            p = jnp.where((row & (SAMPLE_ROWS - 1)) >= SAMPLE_ROWS - 2, st_ref[:, sl], p)
            r1 = pltpu.roll(p, 1, axis=0)
            r2 = pltpu.roll(p, 2, axis=0)
        else:
            r1 = pltpu.roll(p, 1, axis=0)
            r2 = pltpu.roll(p, 2, axis=0)
            h = halo_s[:, sl]
            r1 = jnp.where(row == 0, h[7:8], r1)
            r2 = jnp.where(row == 0, h[6:7], jnp.where(row == 1, h[7:8], r2))
            halo_s[:, sl] = p[tm - 8:tm]
            p_ref[:, sl] = p[tm - 8:tm]
        conv = cw_ref[0:1, sl] * r2 + cw_ref[1:2, sl] * r1 + cw_ref[2:3, sl] * p
        c_s[:, sl] = (cb * conv).astype(BF16)

    def natural(ref):
        dil = ref.shape[0]
        if dil == 1:
            return ref[0]
        s_ref = scratch.pop()
        for r in range(dil):
            for t in range(LANE_TILES):
                s_ref[t, pl.ds(r, tm // dil, stride=dil), :] = ref[r, :, t * LANES:(t + 1) * LANES]
        return jnp.concatenate([s_ref[t] for t in range(LANE_TILES)], axis=1)

    o = [natural(attn_refs[2 * g]) for g in range(N_GROUPS)]
    lse = [natural(attn_refs[2 * g + 1]) for g in range(N_GROUPS)]
    mx = jnp.maximum(jnp.maximum(lse[0], lse[1]), lse[2])
    e = [jnp.exp2(l - mx) for l in lse]
    a = (e[0] * o[0] + e[1] * o[1] + e[2] * o[2]) / (e[0] + e[1] + e[2])
    aproj = jnp.dot(a.astype(BF16), wao_ref[...], preferred_element_type=F32)
    cproj = jnp.dot(c_s[...], wco_ref[...], preferred_element_type=F32)
    for c in range(D_MODEL // GATE_CHUNK):
        sl = slice(c * GATE_CHUNK, (c + 1) * GATE_CHUNK)
        za = jnp.dot(u_s[...], wg_ref[:, sl], preferred_element_type=F32) + bg_ref[:, sl]
        zc = (jnp.dot(u_s[...], wg_ref[:, D_MODEL + c * GATE_CHUNK:D_MODEL + (c + 1) * GATE_CHUNK],
                      preferred_element_type=F32)
              + bg_ref[:, D_MODEL + c * GATE_CHUNK:D_MODEL + (c + 1) * GATE_CHUNK])
        m = jax.nn.sigmoid(za) * aproj[:, sl] + jax.nn.sigmoid(zc) * cproj[:, sl]
        m_s[:, sl] = m.astype(BF16)
    out_ref[...] = x_ref[...] + jnp.dot(m_s[...], wo_ref[...], preferred_element_type=F32)


def _mixer_call(x, gains, wc, wg, bg, cw, wco, wao, wo, state, attn, layer, tm, seq_rows, name):
    n = x.shape[0]
    sample = state is not None
    tiles_per_seq = seq_rows // tm
    row_spec = pl.BlockSpec((tm, D_MODEL), lambda i: (i, 0))
    in_specs = [
        row_spec,
        pl.BlockSpec((None, 1, D_MODEL), lambda i: (layer, 0, 0)),
        _resident((None, D_MODEL, 3 * CONV_DIM), lambda i: (layer, 0, 0)),
        _resident((None, D_MODEL, 2 * D_MODEL), lambda i: (layer, 0, 0)),
        pl.BlockSpec((None, 1, 2 * D_MODEL), lambda i: (layer, 0, 0)),
        pl.BlockSpec((None, 3, CONV_DIM), lambda i: (layer, 0, 0)),
        _resident((None, CONV_DIM, D_MODEL), lambda i: (layer, 0, 0)),
        _resident((None, GROUP_COLS, D_MODEL), lambda i: (layer, 0, 0)),
        _resident((None, D_MODEL, D_MODEL), lambda i: (layer, 0, 0)),
    ]
    args = [x, gains, wc, wg, bg, cw, wco, wao, wo]
    scratch = [pltpu.VMEM((tm, D_MODEL), BF16), pltpu.VMEM((tm, CONV_DIM), BF16),
               pltpu.VMEM((tm, D_MODEL), BF16)]
    if sample:
        in_specs.append(row_spec)
        args.append(state)
        p_spec = row_spec
        p_shape = jax.ShapeDtypeStruct((n, CONV_DIM), F32)
    else:
        p_spec = pl.BlockSpec((8, CONV_DIM), lambda i: (i // tiles_per_seq, 0))
        p_shape = jax.ShapeDtypeStruct((n // seq_rows * 8, CONV_DIM), F32)
        scratch.append(pltpu.VMEM((8, CONV_DIM), F32))
    flat = [a for pair in attn for a in pair]
    in_specs += [pl.BlockSpec((None, a.shape[1], tm // a.shape[1], GROUP_COLS),
                              lambda i: (i // tiles_per_seq, 0, i % tiles_per_seq, 0))
                 for a in flat]
    scratch += [pltpu.VMEM((LANE_TILES, tm, LANES), F32)] * sum(a.shape[1] > 1 for a in flat)
    return pl.pallas_call(
        functools.partial(_mixer_kernel, sample=sample, tiles_per_seq=tiles_per_seq),
        grid=(n // tm,),
        in_specs=in_specs,
        out_specs=[row_spec, p_spec],
        out_shape=[jax.ShapeDtypeStruct((n, D_MODEL), F32), p_shape],
        scratch_shapes=scratch,
        compiler_params=_params(1),
        name=name,
    )(*args, *flat)


def _attn_prompt_kernel(qe_ref, qo_ref, kp_ref, kc_ref, vp_ref, vc_ref, o_ref, l_ref):
    n = pl.program_id(2)
    qi = lax.broadcasted_iota(jnp.int32, (QBLOCK, 2 * QBLOCK), 0)
    col = lax.broadcasted_iota(jnp.int32, (QBLOCK, 2 * QBLOCK), 1)
    cur_ok = (col >= QBLOCK) & (col - QBLOCK <= qi)
    bias_inner = jnp.where(cur_ok | ((col < QBLOCK) & (col >= qi)), 0.0, NEG)
    bias_first = jnp.where(cur_ok | ((col < QBLOCK) & (col >= qi + jnp.where(n > 0, 0, QBLOCK))),
                           0.0, NEG)
    even = lax.broadcasted_iota(jnp.int32, (QBLOCK, 2 * HEAD_DIM), 1) < HEAD_DIM
    for i in range(ATTN_BLOCKS):
        rows = slice(i * QBLOCK, (i + 1) * QBLOCK)
        if i == 0:
            k = jnp.concatenate([kp_ref[...], kc_ref[rows]], axis=0)
            v = jnp.concatenate([vp_ref[...], vc_ref[rows]], axis=0)
            bias = bias_first
        else:
            k = kc_ref[(i - 1) * QBLOCK:(i + 1) * QBLOCK]
            v = vc_ref[(i - 1) * QBLOCK:(i + 1) * QBLOCK]
            bias = bias_inner
        for j in range(HEADS // 2):
            sl = slice(2 * HEAD_DIM * j, 2 * HEAD_DIM * (j + 1))
            halves = []
            for q_ref in (qe_ref, qo_ref):
                s = lax.dot_general(q_ref[rows, sl], k[:, sl], (((1,), (1,)), ((), ())),
                                    preferred_element_type=F32) + bias
                m = jnp.max(s, axis=1, keepdims=True)
                p = jnp.exp2(s - m)
                l = jnp.sum(p, axis=1, keepdims=True)
                pv = jnp.dot(p.astype(BF16), v[:, sl], preferred_element_type=F32)
                halves.append((pv / l, jnp.broadcast_to(m + jnp.log2(l), pv.shape)))
            o_ref[rows, sl] = jnp.where(even, halves[0][0], halves[1][0])
            l_ref[rows, sl] = jnp.where(even, halves[0][1], halves[1][1])


def _attn_prompt_call(qkv, name):
    batch, dil, length, _ = qkv.shape
    step_rows = ATTN_BLOCKS * QBLOCK
    nb = length // step_rows

    def spec(part, prev=False):
        if prev:
            return pl.BlockSpec((None, None, QBLOCK, GROUP_COLS),
                                lambda b, r, n: (b, r, jnp.maximum(ATTN_BLOCKS * n - 1, 0), part))
        return pl.BlockSpec((None, None, step_rows, GROUP_COLS), lambda b, r, n: (b, r, n, part))

    out_spec = pl.BlockSpec((None, None, step_rows, GROUP_COLS), lambda b, r, n: (b, r, n, 0))
    out_shape = jax.ShapeDtypeStruct((batch, dil, length, GROUP_COLS), F32)
    return pl.pallas_call(
        _attn_prompt_kernel,
        grid=(batch, dil, nb),
        in_specs=[spec(0), spec(1), spec(2, True), spec(2), spec(3, True), spec(3)],
        out_specs=[out_spec, out_spec],
        out_shape=[out_shape, out_shape],
        compiler_params=_params(3),
        name=name,
    )(qkv, qkv, qkv, qkv, qkv, qkv)


def _attn_sample_kernel(*refs, dils, dec):
    n_g = len(dils)
    for g, dil in enumerate(dils):
        q_ref, kn_ref, vn_ref = refs[3 * g:3 * g + 3]
        c_ref = refs[3 * n_g + g]
        o_ref, l_ref = refs[4 * n_g + 2 * g:4 * n_g + 2 * g + 2]
        _attn_sample_group(q_ref, kn_ref, vn_ref, c_ref, o_ref, l_ref, dil, dec)


def _attn_sample_group(q_ref, kn_ref, vn_ref, c_ref, o_ref, l_ref, dil, dec):
    lb = c_ref.shape[-1]
    q = q_ref[...]
    sub = lax.broadcasted_iota(jnp.int32, (SAMPLE_ROWS, GROUP_COLS), 0)
    lane = lax.broadcasted_iota(jnp.int32, (SAMPLE_ROWS, GROUP_COLS), 1)
    diag = sub == lane // HEAD_DIM
    qb = jnp.concatenate(
        [jnp.where(diag, jnp.broadcast_to(q[t:t + 1], diag.shape), 0.0) for t in range(dec)],
        axis=0).astype(BF16)
    rows = dec * HEADS
    kt = c_ref[0].astype(BF16)
    vt = c_ref[1].astype(BF16)
    s = jnp.dot(qb, kt, preferred_element_type=F32)
    t_of = lax.broadcasted_iota(jnp.int32, (rows, lb), 0) // HEADS
    diff = lax.broadcasted_iota(jnp.int32, (rows, lb), 1) - t_of
    valid = (diff >= 0) & ((diff & (dil - 1)) == 0)
    s = jnp.where(valid, s, NEG)
    m = jnp.max(s, axis=1, keepdims=True)

    qbf = qb.astype(F32)
    knr = kn_ref[...].astype(BF16).astype(F32)
    vnr = vn_ref[...].astype(BF16).astype(F32)
    t_col = lax.broadcasted_iota(jnp.int32, (rows, 1), 0) // HEADS
    s_new = []
    for t2 in range(dec):
        s2 = jnp.sum(qbf * knr[t2:t2 + 1], axis=1, keepdims=True)
        d2 = t_col - t2
        s2 = jnp.where((d2 >= 0) & ((d2 & (dil - 1)) == 0), s2, NEG)
        s_new.append(s2)
        m = jnp.maximum(m, s2)
    p = jnp.exp(s - m)
    l = jnp.sum(p, axis=1, keepdims=True)
    acc = lax.dot_general(p.astype(BF16), vt, (((1,), (1,)), ((), ())),
                          preferred_element_type=F32)
    for t2 in range(dec):
        pn = jnp.exp(s_new[t2] - m)
        l = l + pn
        acc = acc + pn * vnr[t2:t2 + 1]
    o = acc / l
    lse = jnp.broadcast_to((m + jnp.log(l)) * LOG2_E, o.shape)
    o_rows, l_rows = [], []
    for t in range(dec):
        blk = slice(t * HEADS, (t + 1) * HEADS)
        o_rows.append(jnp.sum(jnp.where(diag, o[blk], 0.0), axis=0, keepdims=True))
        l_rows.append(jnp.sum(jnp.where(diag, lse[blk], 0.0), axis=0, keepdims=True))
    pad = jnp.zeros((SAMPLE_ROWS - dec, GROUP_COLS), F32)
    o_ref[...] = jnp.concatenate(o_rows + [pad], axis=0)
    l_ref[...] = jnp.concatenate(l_rows + [pad], axis=0)


def _attn_sample_call(qkv32, caches, layer, dils, dec, name):
    n_seq = caches[0].shape[1]
    n = qkv32[0].shape[0]
    in_specs, args = [], []
    for a in qkv32:
        for kind in range(3):
            in_specs.append(pl.BlockSpec((SAMPLE_ROWS, GROUP_COLS), lambda s, kind=kind: (s, kind)))
            args.append(a)
    for c in caches:
        in_specs.append(pl.BlockSpec((None, None, 2, GROUP_COLS, c.shape[-1]),
                                     lambda s: (layer, s, 0, 0, 0)))
        args.append(c)
    out_row = pl.BlockSpec((SAMPLE_ROWS, GROUP_COLS), lambda s: (s, 0))
    outs = pl.pallas_call(
        functools.partial(_attn_sample_kernel, dils=dils, dec=dec),
        grid=(n_seq,),
        in_specs=in_specs,
        out_specs=[out_row] * (2 * len(dils)),
        out_shape=[jax.ShapeDtypeStruct((n, GROUP_COLS), F32)] * (2 * len(dils)),
        compiler_params=_params(1),
        name=name,
    )(*args)
    return [(outs[2 * g], outs[2 * g + 1]) for g in range(len(dils))]


def kernel(x_prompt, x_sample, cache_kv1, cache_kv2, cache_kv3, state_conv, ffn1_norm, ffn1_w_in,
           ffn1_w_out, mix_norm, w_in, b_gate, conv_w, w_attn_out, w_conv_out, w_out, ffn2_norm,
           ffn2_w_in, ffn2_w_out, final_norm):
    batch, seq, _ = x_prompt.shape
    n_seq, dec, _ = x_sample.shape
    depth = w_in.shape[0]
    caches = (cache_kv1, cache_kv2, cache_kv3)
    assert dec <= SAMPLE_ROWS - 2 and seq % (GROUPS[-1][1] * QBLOCK * ATTN_BLOCKS) == 0
    for cache, (win, dil) in zip(caches, GROUPS):
        assert cache.shape[2] == win and win == SPAN * dil and (dil == 1 or dec <= dil)

    f1_wi, f1_wo = ffn1_w_in.astype(BF16), ffn1_w_out.astype(BF16)
    f2_wi, f2_wo = ffn2_w_in.astype(BF16), ffn2_w_out.astype(BF16)
    w_qkv, w_conv, w_gate = _cast_split_call(w_in, (QKV_DIM, 3 * CONV_DIM, 2 * D_MODEL), LANES,
                                             "cast_w_in")
    w_ao = w_attn_out.astype(BF16)
    w_co = w_conv_out.astype(BF16)
    w_o = w_out.astype(BF16)
    g1 = ffn1_norm.reshape(depth, 1, D_MODEL)
    gm = mix_norm.reshape(depth, 1, D_MODEL)
    g2 = ffn2_norm.reshape(depth, 1, D_MODEL)
    gf = final_norm.reshape(1, D_MODEL)
    bg = b_gate.reshape(depth, 1, 2 * D_MODEL)

    caches_t = [c.transpose(0, 1, 3, 4, 5, 2).reshape(depth, n_seq, 2, GROUP_COLS, c.shape[2])
                for c in caches]

    xp = x_prompt.reshape(batch * seq, D_MODEL)
    xs = jnp.pad(x_sample, ((0, 0), (0, SAMPLE_ROWS - dec), (0, 0))).reshape(n_seq * SAMPLE_ROWS, D_MODEL)
    ns = xs.shape[0]
    tail_rows = GROUPS[-1][0]
    dils = tuple(dil for _, dil in GROUPS)

    kv_prompt = [[] for _ in GROUPS]
    conv_prompt, conv_sample = [], []
    s_tails = []
    for l in range(depth):
        last = l == depth - 1
        xs = _ffn_call(xs, g1, f1_wi, f1_wo, gf, l, ns, False, f"s_ffn1_{l}")[0]
        s_qkv = _qkv_call(xs, gm, w_qkv, l, ns, ns, ns, (1,) * N_GROUPS, f"s_qkv_{l}")[N_GROUPS:]
        s_tail = [a.reshape(ns, 3 * GROUP_COLS) for a in s_qkv]
        s_tails.append(s_tail)
        st = jnp.pad(state_conv[l], ((0, 0), (SAMPLE_ROWS - 2, 0), (0, 0)))
        st = jnp.roll(st, -1, axis=0).reshape(ns, CONV_DIM)
        s_attn = [(o.reshape(1, 1, ns, GROUP_COLS), lse.reshape(1, 1, ns, GROUP_COLS))
                  for o, lse in _attn_sample_call(s_tail, caches_t, l, dils, dec, f"s_attn_{l}")]
        xs, s_p = _mixer_call(xs, gm, w_conv, w_gate, bg, conv_w, w_co, w_ao, w_o, st, s_attn, l,
                              ns, ns, f"s_mix_{l}")
        conv_sample.append(s_p.reshape(n_seq, SAMPLE_ROWS, CONV_DIM)[:, dec - 2:dec])
        xs = _ffn_call(xs, g2, f2_wi, f2_wo, gf, l, ns, last, f"s_ffn2_{l}")[0]

    new_caches = None
    for l in range(depth):
        last = l == depth - 1
        xp, *new_caches = _ffn_call(xp, g1, f1_wi, f1_wo, gf, l, ROW_TILE, False, f"p_ffn1_{l}",
                                    roll=(s_tails[l], caches_t, new_caches, 0, dec))
        p_qkv = _qkv_call(xp, gm, w_qkv, l, ROW_TILE, seq, tail_rows, dils, f"p_qkv_{l}")
        p_attn = []
        for g, (win, dil) in enumerate(GROUPS):
            p_attn.append(_attn_prompt_call(p_qkv[g], f"p_attn{g}_{l}"))
            keep = min(win, seq)
            tail = p_qkv[N_GROUPS + g].transpose(0, 2, 1, 3).reshape(batch, tail_rows, 3 * GROUP_COLS)
            k_t = tail[:, tail_rows - keep:, GROUP_COLS:2 * GROUP_COLS]
            v_t = tail[:, tail_rows - keep:, 2 * GROUP_COLS:]
            kv_prompt[g].append(jnp.stack([k_t, v_t], axis=2).reshape(batch, keep, 2, HEADS, HEAD_DIM))
        xp, p_p = _mixer_call(xp, gm, w_conv, w_gate, bg, conv_w, w_co, w_ao, w_o, None, p_attn, l,
                              ROW_TILE, seq, f"p_mix_{l}")
        conv_prompt.append(p_p.reshape(batch, 8, CONV_DIM)[:, 6:8])
        xp, *new_caches = _ffn_call(xp, g2, f2_wi, f2_wo, gf, l, ROW_TILE, last, f"p_ffn2_{l}",
                                    roll=(s_tails[l], caches_t, new_caches, n_seq // 2, dec))

    y_prompt = xp.reshape(batch, seq, D_MODEL)
    y_sample = xs.reshape(n_seq, SAMPLE_ROWS, D_MODEL)[:, :dec]
    kv_sample = [c.reshape(depth, n_seq, 2, HEADS, HEAD_DIM, c.shape[-1]).transpose(0, 1, 5, 2, 3, 4)
                 for c in new_caches]
    return (y_prompt, y_sample,
            jnp.stack(kv_prompt[0]), jnp.stack(kv_prompt[1]), jnp.stack(kv_prompt[2]),
            jnp.stack(conv_prompt),
            kv_sample[0], kv_sample[1], kv_sample[2],
            jnp.stack(conv_sample))
```

```python
import functools

import jax
import jax.numpy as jnp
from jax import lax
from jax.experimental import pallas as pl
from jax.experimental.pallas import tpu as pltpu

F32 = jnp.float32
BF16 = jnp.bfloat16

D_MODEL = 1024
D_FF = 2816
HEAD_DIM = 64
HEADS = 8
GROUP_COLS = HEADS * HEAD_DIM
HALF_COLS = GROUP_COLS // 2
GROUPS = ((128, 1), (512, 4), (2048, 16))
N_GROUPS = len(GROUPS)
SPAN = 128
ATTN_DIM = N_GROUPS * GROUP_COLS
QKV_DIM = 3 * ATTN_DIM
CONV_DIM = D_MODEL
RMS_EPS = 1e-6
ATTN_SCALE = HEAD_DIM ** -0.5
LOG2_E = 1.4426950408889634
LN_2 = 0.6931471805599453
NEG = -1e30

FF_CHUNK = 256
N_FF_CHUNKS = D_FF // FF_CHUNK
WINDOW_AT_CHUNK = {1: 2, 5: 1, 8: 0}
QBLOCK = 128
ATTN_BLOCKS = 4
LANES = 128
LANE_TILES = GROUP_COLS // LANES
SAMPLE_ROWS = 8
V7X_VMEM_LIMIT_BYTES = 56 * 1024 * 1024
ROW_TILE = 512


def _params(n_axes):
    return pltpu.CompilerParams(
        dimension_semantics=("arbitrary",) * n_axes,
        vmem_limit_bytes=V7X_VMEM_LIMIT_BYTES)


def _resident(block_shape, index_map):
    return pl.BlockSpec(block_shape, index_map, pipeline_mode=pl.Buffered(1))


def _rms_inv(x):
    return lax.rsqrt(jnp.mean(x * x, axis=-1, keepdims=True) + RMS_EPS)


def _cast_split_kernel(w_ref, *out_refs):
    off = 0
    for out in out_refs:
        width = out.shape[-1]
        out[...] = w_ref[:, off:off + width].astype(BF16)
        off += width


def _cast_split_call(w, widths, rows, name):
    depth, r, c = w.shape
    assert sum(widths) == c and r % rows == 0
    return pl.pallas_call(
        _cast_split_kernel,
        grid=(depth, r // rows),
        in_specs=[pl.BlockSpec((None, rows, c), lambda l, i: (l, i, 0))],
        out_specs=[pl.BlockSpec((None, rows, wd), lambda l, i: (l, i, 0)) for wd in widths],
        out_shape=[jax.ShapeDtypeStruct((depth, r, wd), BF16) for wd in widths],
        compiler_params=_params(2),
        name=name,
    )(w)


def _roll_window(c_ref, new_rows, out_ref, dec):
    lb = c_ref.shape[1]
    rolled = pltpu.roll(c_ref[...], lb - dec, axis=1)
    out_ref[...] = rolled
    sub = lax.broadcasted_iota(jnp.int32, new_rows.shape, 0)
    shifted = jnp.where(sub >= SAMPLE_ROWS - dec,
                        pltpu.roll(new_rows, SAMPLE_ROWS - dec, axis=0), 0.0)
    padded = jnp.concatenate(
        [jnp.zeros((LANES - SAMPLE_ROWS, new_rows.shape[1]), F32), shifted], axis=0)
    new_t = padded.T
    lane_t = lax.broadcasted_iota(jnp.int32, new_t.shape, 1)
    out_ref[:, lb - LANES:] = jnp.where(lane_t >= LANES - dec, new_t, rolled[:, lb - LANES:])


def _ffn_kernel(*refs, final, dec, n_alias):
    x_ref, g_ref, wi_ref, wo_ref, fg_ref = refs[:5]
    refs = refs[5:]
    if dec:
        new_refs, cache_refs = refs[:N_GROUPS], refs[N_GROUPS:2 * N_GROUPS]
        refs = refs[2 * N_GROUPS + n_alias:]
        o_ref, roll_refs, refs = refs[0], refs[1:1 + N_GROUPS], refs[1 + N_GROUPS:]
        half_refs, refs = refs[:N_GROUPS], refs[N_GROUPS:]
    else:
        o_ref, refs = refs[0], refs[1:]
    h_s, acc_s = refs
    if dec:
        is_hi = (lax.broadcasted_iota(jnp.int32, (SAMPLE_ROWS, HALF_COLS), 0) * 0
                 + pl.program_id(0) % 2) == 1

        def stream_window(g):
            kn, vn = (jnp.where(is_hi, new_refs[g][:, k * GROUP_COLS + HALF_COLS:(k + 1) * GROUP_COLS],
                                new_refs[g][:, k * GROUP_COLS:k * GROUP_COLS + HALF_COLS])
                      for k in (1, 2))
            _roll_window(cache_refs[g].at[0], kn, roll_refs[g].at[0], dec)
            _roll_window(cache_refs[g].at[1], vn, roll_refs[g].at[1], dec)
            half_refs[g][...] = cache_refs[g][...].astype(BF16)

    x = x_ref[...]
    h_s[...] = ((x * _rms_inv(x)) * g_ref[...]).astype(BF16)
    for c in range(N_FF_CHUNKS):
        if dec and c in WINDOW_AT_CHUNK:
            stream_window(WINDOW_AT_CHUNK[c])
        lo = c * FF_CHUNK
        gate = jnp.dot(h_s[...], wi_ref[:, lo:lo + FF_CHUNK], preferred_element_type=F32)
        up = jnp.dot(h_s[...], wi_ref[:, D_FF + lo:D_FF + lo + FF_CHUNK],
                     preferred_element_type=F32)
        act = ((gate * jax.nn.sigmoid(gate)) * up).astype(BF16)
        part = jnp.dot(act, wo_ref[lo:lo + FF_CHUNK, :], preferred_element_type=F32)
        if c == 0:
            acc_s[...] = part
        else:
            acc_s[...] += part
    y = x_ref[...] + 0.5 * acc_s[...]
    if final:
        y = (y * _rms_inv(y)) * fg_ref[...]
    o_ref[...] = y


def _ffn_call(x, gains, wi, wo, final_gain, layer, tm, final, name, roll=None):
    n = x.shape[0]
    in_specs = [
        pl.BlockSpec((tm, D_MODEL), lambda i: (i, 0)),
        pl.BlockSpec((None, 1, D_MODEL), lambda i: (layer, 0, 0)),
        _resident((None, D_MODEL, 2 * D_FF), lambda i: (layer, 0, 0)),
        _resident((None, D_FF, D_MODEL), lambda i: (layer, 0, 0)),
        pl.BlockSpec((1, D_MODEL), lambda i: (0, 0)),
    ]
    args = [x, gains, wi, wo, final_gain]
    out_specs = [pl.BlockSpec((tm, D_MODEL), lambda i: (i, 0))]
    out_shape = [jax.ShapeDtypeStruct((n, D_MODEL), F32)]
    aliases, dec, n_alias = {}, 0, 0
    scratch = [pltpu.VMEM((tm, D_MODEL), BF16), pltpu.VMEM((tm, D_MODEL), F32)]
    if roll is not None:
        new_rows, caches, prev_outs, seq_base, dec = roll
        assert 2 * (caches[0].shape[1] // 2) == n // tm
        in_specs += [pl.BlockSpec((SAMPLE_ROWS, 3 * GROUP_COLS), lambda i: (seq_base + i // 2, 0))
                     for _ in new_rows]
        args += list(new_rows)
        slab_specs = [pl.BlockSpec((None, None, 2, HALF_COLS, c.shape[-1]),
                                   lambda i: (layer, seq_base + i // 2, 0, i % 2, 0)) for c in caches]
        in_specs += slab_specs
        args += list(caches)
        if prev_outs is not None:
            n_alias = N_GROUPS
            for g, prev in enumerate(prev_outs):
                aliases[len(args)] = 1 + g
                in_specs.append(pl.BlockSpec(memory_space=pl.ANY))
                args.append(prev)
        out_specs += slab_specs
        out_shape += [jax.ShapeDtypeStruct(c.shape, F32) for c in caches]
        out_specs += [pl.BlockSpec((None, 2, HALF_COLS, c.shape[-1]), lambda i: (i // 2, 0, i % 2, 0))
                      for c in caches]
        out_shape += [jax.ShapeDtypeStruct((n // tm // 2, 2, GROUP_COLS, c.shape[-1]), BF16)
                      for c in caches]
    return pl.pallas_call(
        functools.partial(_ffn_kernel, final=final, dec=dec, n_alias=n_alias),
        grid=(n // tm,),
        in_specs=in_specs,
        out_specs=out_specs,
        out_shape=out_shape,
        input_output_aliases=aliases,
        scratch_shapes=scratch,
        compiler_params=_params(1),
        name=name,
    )(*args)


def _qkv_kernel(x_ref, g_ref, w_ref, *refs, dils):
    n_g = len(dils)
    bf_refs, f32_refs = refs[:n_g], refs[n_g:2 * n_g]
    u_s, slab_s = refs[2 * n_g], refs[2 * n_g + 1]
    perm_refs = list(refs[2 * n_g + 2:])
    tm = x_ref.shape[0]
    x = x_ref[...]
    u = (x * _rms_inv(x)) * g_ref[...]
    u_s[...] = u.astype(BF16)
    if perm_refs:
        for t in range(D_MODEL // LANES):
            slab_s[t] = u[:, t * LANES:(t + 1) * LANES]
    for g, dil in enumerate(dils):
        n = tm // dil
        lhs_ref = u_s
        if dil > 1:
            lhs_ref = perm_refs.pop()
            for r in range(dil):
                for t in range(D_MODEL // LANES):
                    lhs_ref[r * n:(r + 1) * n, t * LANES:(t + 1) * LANES] = (
                        slab_s[t, pl.ds(r, n, stride=dil), :].astype(BF16))
        even = (lax.broadcasted_iota(jnp.int32, (n, GROUP_COLS), 1) & (2 * HEAD_DIM - 1)) < HEAD_DIM
        for kind in range(3):
            col = (kind * n_g + g) * GROUP_COLS
            z = jnp.dot(lhs_ref[...], w_ref[:, col:col + GROUP_COLS], preferred_element_type=F32)
            if kind == 0:
                z = z * ATTN_SCALE
            osl = slice(kind * GROUP_COLS, (kind + 1) * GROUP_COLS)
            for r in range(dil):
                zr = z[r * n:(r + 1) * n]
                f32_refs[g][r, :, osl] = zr
                if kind == 0:
                    zq = zr * LOG2_E
                    bf_refs[g][r, :, :GROUP_COLS] = jnp.where(even, zq, 0.0).astype(BF16)
                    bf_refs[g][r, :, GROUP_COLS:2 * GROUP_COLS] = jnp.where(even, 0.0, zq).astype(BF16)
                else:
                    bf_refs[g][r, :, (kind + 1) * GROUP_COLS:(kind + 2) * GROUP_COLS] = zr.astype(BF16)


def _qkv_call(x, gains, w, layer, tm, seq_rows, tail_rows, dils, name):
    n = x.shape[0]
    tiles_per_seq = seq_rows // tm
    tail_tiles = tail_rows // tm
    n_seq = n // seq_rows

    def tail_map(i):
        return (i // tiles_per_seq, 0,
                jnp.maximum(i % tiles_per_seq - (tiles_per_seq - tail_tiles), 0), 0)

    block = lambda dil, parts=3: (None, dil, tm // dil, parts * GROUP_COLS)
    specs = [pl.BlockSpec(block(dil, 4), lambda i: (i // tiles_per_seq, 0, i % tiles_per_seq, 0))
             for dil in dils]
    specs += [pl.BlockSpec(block(dil), tail_map) for dil in dils]
    shapes = [jax.ShapeDtypeStruct((n_seq, dil, seq_rows // dil, 4 * GROUP_COLS), BF16) for dil in dils]
    shapes += [jax.ShapeDtypeStruct((n_seq, dil, tail_rows // dil, 3 * GROUP_COLS), F32) for dil in dils]
    n_perm = sum(dil > 1 for dil in dils)
    return pl.pallas_call(
        functools.partial(_qkv_kernel, dils=dils),
        grid=(n // tm,),
        in_specs=[
            pl.BlockSpec((tm, D_MODEL), lambda i: (i, 0)),
            pl.BlockSpec((None, 1, D_MODEL), lambda i: (layer, 0, 0)),
            _resident((None, D_MODEL, QKV_DIM), lambda i: (layer, 0, 0)),
        ],
        out_specs=specs,
        out_shape=shapes,
        scratch_shapes=[pltpu.VMEM((tm, D_MODEL), BF16),
                        pltpu.VMEM((D_MODEL // LANES, tm, LANES), F32)]
                       + [pltpu.VMEM((tm, D_MODEL), BF16)] * n_perm,
        compiler_params=_params(1),
        name=name,
    )(x, gains, w)


CONV_CHUNK = 256
GATE_CHUNK = 512


def _mixer_kernel(*refs, sample, tiles_per_seq):
    x_ref, g_ref, wc_ref, wg_ref, bg_ref, cw_ref, wco_ref, wao_ref, wo_ref = refs[:9]
    refs = refs[9:]
    if sample:
        st_ref, refs = refs[0], refs[1:]
    attn_refs, refs = refs[:2 * N_GROUPS], refs[2 * N_GROUPS:]
    out_ref, p_ref, u_s, c_s, m_s = refs[:5]
    scratch = list(refs[5:])
    if not sample:
        halo_s = scratch.pop(0)
    tm = x_ref.shape[0]
    x = x_ref[...]
    u_s[...] = ((x * _rms_inv(x)) * g_ref[...]).astype(BF16)
    row = lax.broadcasted_iota(jnp.int32, (tm, CONV_CHUNK), 0)

    if not sample:
        @pl.when(pl.program_id(0) % tiles_per_seq == 0)
        def _():
            halo_s[...] = jnp.zeros_like(halo_s)

    for c in range(CONV_DIM // CONV_CHUNK):
        sl = slice(c * CONV_CHUNK, (c + 1) * CONV_CHUNK)
        u = u_s[...]
        cb = jnp.dot(u, wc_ref[:, sl], preferred_element_type=F32)
        cc = jnp.dot(u, wc_ref[:, CONV_DIM + c * CONV_CHUNK:CONV_DIM + (c + 1) * CONV_CHUNK],
                     preferred_element_type=F32)
        ch = jnp.dot(u, wc_ref[:, 2 * CONV_DIM + c * CONV_CHUNK:2 * CONV_DIM + (c + 1) * CONV_CHUNK],
                     preferred_element_type=F32)
        p = cc * ch
        if sample:
            p_ref[:, sl] = p
            p = jnp.where((row & (SAMPLE_ROWS - 1)) >= SAMPLE_ROWS - 2, st_ref[:, sl], p)
            r1 = pltpu.roll(p, 1, axis=0)
            r2 = pltpu.roll(p, 2, axis=0)
        else:
            r1 = pltpu.roll(p, 1, axis=0)
            r2 = pltpu.roll(p, 2, axis=0)
            h = halo_s[:, sl]
            r1 = jnp.where(row == 0, h[7:8], r1)
            r2 = jnp.where(row == 0, h[6:7], jnp.where(row == 1, h[7:8], r2))
            halo_s[:, sl] = p[tm - 8:tm]
            p_ref[:, sl] = p[tm - 8:tm]
        conv = cw_ref[0:1, sl] * r2 + cw_ref[1:2, sl] * r1 + cw_ref[2:3, sl] * p
        c_s[:, sl] = (cb * conv).astype(BF16)

    def natural(ref):
        dil = ref.shape[0]
        if dil == 1:
            return ref[0]
        s_ref = scratch.pop()
        for r in range(dil):
            for t in range(LANE_TILES):
                s_ref[t, pl.ds(r, tm // dil, stride=dil), :] = ref[r, :, t * LANES:(t + 1) * LANES]
        return jnp.concatenate([s_ref[t] for t in range(LANE_TILES)], axis=1)

    o = [natural(attn_refs[2 * g]) for g in range(N_GROUPS)]
    lse = [natural(attn_refs[2 * g + 1]) for g in range(N_GROUPS)]
    mx = jnp.maximum(jnp.maximum(lse[0], lse[1]), lse[2])
    e = [jnp.exp2(l - mx) for l in lse]
    a = (e[0] * o[0] + e[1] * o[1] + e[2] * o[2]) / (e[0] + e[1] + e[2])
    aproj = jnp.dot(a.astype(BF16), wao_ref[...], preferred_element_type=F32)
    cproj = jnp.dot(c_s[...], wco_ref[...], preferred_element_type=F32)
    for c in range(D_MODEL // GATE_CHUNK):
        sl = slice(c * GATE_CHUNK, (c + 1) * GATE_CHUNK)
        za = jnp.dot(u_s[...], wg_ref[:, sl], preferred_element_type=F32) + bg_ref[:, sl]
        zc = (jnp.dot(u_s[...], wg_ref[:, D_MODEL + c * GATE_CHUNK:D_MODEL + (c + 1) * GATE_CHUNK],
                      preferred_element_type=F32)
              + bg_ref[:, D_MODEL + c * GATE_CHUNK:D_MODEL + (c + 1) * GATE_CHUNK])
        m = jax.nn.sigmoid(za) * aproj[:, sl] + jax.nn.sigmoid(zc) * cproj[:, sl]
        m_s[:, sl] = m.astype(BF16)
    out_ref[...] = x_ref[...] + jnp.dot(m_s[...], wo_ref[...], preferred_element_type=F32)


def _mixer_call(x, gains, wc, wg, bg, cw, wco, wao, wo, state, attn, layer, tm, seq_rows, name):
    n = x.shape[0]
    sample = state is not None
    tiles_per_seq = seq_rows // tm
    row_spec = pl.BlockSpec((tm, D_MODEL), lambda i: (i, 0))
    in_specs = [
        row_spec,
        pl.BlockSpec((None, 1, D_MODEL), lambda i: (layer, 0, 0)),
        _resident((None, D_MODEL, 3 * CONV_DIM), lambda i: (layer, 0, 0)),
        _resident((None, D_MODEL, 2 * D_MODEL), lambda i: (layer, 0, 0)),
        pl.BlockSpec((None, 1, 2 * D_MODEL), lambda i: (layer, 0, 0)),
        pl.BlockSpec((None, 3, CONV_DIM), lambda i: (layer, 0, 0)),
        _resident((None, CONV_DIM, D_MODEL), lambda i: (layer, 0, 0)),
        _resident((None, GROUP_COLS, D_MODEL), lambda i: (layer, 0, 0)),
        _resident((None, D_MODEL, D_MODEL), lambda i: (layer, 0, 0)),
    ]
    args = [x, gains, wc, wg, bg, cw, wco, wao, wo]
    scratch = [pltpu.VMEM((tm, D_MODEL), BF16), pltpu.VMEM((tm, CONV_DIM), BF16),
               pltpu.VMEM((tm, D_MODEL), BF16)]
    if sample:
        in_specs.append(row_spec)
        args.append(state)
        p_spec = row_spec
        p_shape = jax.ShapeDtypeStruct((n, CONV_DIM), F32)
    else:
        p_spec = pl.BlockSpec((8, CONV_DIM), lambda i: (i // tiles_per_seq, 0))
        p_shape = jax.ShapeDtypeStruct((n // seq_rows * 8, CONV_DIM), F32)
        scratch.append(pltpu.VMEM((8, CONV_DIM), F32))
    flat = [a for pair in attn for a in pair]
    in_specs += [pl.BlockSpec((None, a.shape[1], tm // a.shape[1], GROUP_COLS),
                              lambda i: (i // tiles_per_seq, 0, i % tiles_per_seq, 0))
                 for a in flat]
    scratch += [pltpu.VMEM((LANE_TILES, tm, LANES), F32)] * sum(a.shape[1] > 1 for a in flat)
    return pl.pallas_call(
        functools.partial(_mixer_kernel, sample=sample, tiles_per_seq=tiles_per_seq),
        grid=(n // tm,),
        in_specs=in_specs,
        out_specs=[row_spec, p_spec],
        out_shape=[jax.ShapeDtypeStruct((n, D_MODEL), F32), p_shape],
        scratch_shapes=scratch,
        compiler_params=_params(1),
        name=name,
    )(*args, *flat)


def _attn_prompt_kernel(qe_ref, qo_ref, kp_ref, kc_ref, vp_ref, vc_ref, o_ref, l_ref):
    n = pl.program_id(2)
    qi = lax.broadcasted_iota(jnp.int32, (QBLOCK, 2 * QBLOCK), 0)
    col = lax.broadcasted_iota(jnp.int32, (QBLOCK, 2 * QBLOCK), 1)
    cur_ok = (col >= QBLOCK) & (col - QBLOCK <= qi)
    bias_inner = jnp.where(cur_ok | ((col < QBLOCK) & (col >= qi)), 0.0, NEG)
    bias_first = jnp.where(cur_ok | ((col < QBLOCK) & (col >= qi + jnp.where(n > 0, 0, QBLOCK))),
                           0.0, NEG)
    even = lax.broadcasted_iota(jnp.int32, (QBLOCK, 2 * HEAD_DIM), 1) < HEAD_DIM
    for i in range(ATTN_BLOCKS):
        rows = slice(i * QBLOCK, (i + 1) * QBLOCK)
        if i == 0:
            k = jnp.concatenate([kp_ref[...], kc_ref[rows]], axis=0)
            v = jnp.concatenate([vp_ref[...], vc_ref[rows]], axis=0)
            bias = bias_first
        else:
            k = kc_ref[(i - 1) * QBLOCK:(i + 1) * QBLOCK]
            v = vc_ref[(i - 1) * QBLOCK:(i + 1) * QBLOCK]
            bias = bias_inner
        for j in range(HEADS // 2):
            sl = slice(2 * HEAD_DIM * j, 2 * HEAD_DIM * (j + 1))
            halves = []
            for q_ref in (qe_ref, qo_ref):
                s = lax.dot_general(q_ref[rows, sl], k[:, sl], (((1,), (1,)), ((), ())),
                                    preferred_element_type=F32) + bias
                m = jnp.max(s, axis=1, keepdims=True)
                p = jnp.exp2(s - m)
                l = jnp.sum(p, axis=1, keepdims=True)
                pv = jnp.dot(p.astype(BF16), v[:, sl], preferred_element_type=F32)
                halves.append((pv / l, jnp.broadcast_to(m + jnp.log2(l), pv.shape)))
            o_ref[rows, sl] = jnp.where(even, halves[0][0], halves[1][0])
            l_ref[rows, sl] = jnp.where(even, halves[0][1], halves[1][1])


def _attn_prompt_call(qkv, name):
    batch, dil, length, _ = qkv.shape
    step_rows = ATTN_BLOCKS * QBLOCK
    nb = length // step_rows

    def spec(part, prev=False):
        if prev:
            return pl.BlockSpec((None, None, QBLOCK, GROUP_COLS),
                                lambda b, r, n: (b, r, jnp.maximum(ATTN_BLOCKS * n - 1, 0), part))
        return pl.BlockSpec((None, None, step_rows, GROUP_COLS), lambda b, r, n: (b, r, n, part))

    out_spec = pl.BlockSpec((None, None, step_rows, GROUP_COLS), lambda b, r, n: (b, r, n, 0))
    out_shape = jax.ShapeDtypeStruct((batch, dil, length, GROUP_COLS), F32)
    return pl.pallas_call(
        _attn_prompt_kernel,
        grid=(batch, dil, nb),
        in_specs=[spec(0), spec(1), spec(2, True), spec(2), spec(3, True), spec(3)],
        out_specs=[out_spec, out_spec],
        out_shape=[out_shape, out_shape],
        compiler_params=_params(3),
        name=name,
    )(qkv, qkv, qkv, qkv, qkv, qkv)


def _sample_attn(q, kn, vn, kt, vt, o_ref, l_ref, dil, dec):
    width, lb = kt.shape
    sub = lax.broadcasted_iota(jnp.int32, (SAMPLE_ROWS, width), 0)
    lane = lax.broadcasted_iota(jnp.int32, (SAMPLE_ROWS, width), 1)
    diag = sub == lane // HEAD_DIM
    qb = jnp.concatenate(
        [jnp.where(diag, jnp.broadcast_to(q[t:t + 1], diag.shape), 0.0) for t in range(dec)],
        axis=0).astype(BF16)
    rows = dec * SAMPLE_ROWS
    s = jnp.dot(qb, kt, preferred_element_type=F32)
    t_of = lax.broadcasted_iota(jnp.int32, (rows, lb), 0) // SAMPLE_ROWS
    diff = lax.broadcasted_iota(jnp.int32, (rows, lb), 1) - t_of
    valid = (diff >= 0) & ((diff & (dil - 1)) == 0)
    s = jnp.where(valid, s, NEG)
    m = jnp.max(s, axis=1, keepdims=True)

    qbf = qb.astype(F32)
    knr = kn.astype(BF16).astype(F32)
    vnr = vn.astype(BF16).astype(F32)
    t_col = lax.broadcasted_iota(jnp.int32, (rows, 1), 0) // SAMPLE_ROWS
    s_new = []
    for t2 in range(dec):
        s2 = jnp.sum(qbf * knr[t2:t2 + 1], axis=1, keepdims=True)
        d2 = t_col - t2
        s2 = jnp.where((d2 >= 0) & ((d2 & (dil - 1)) == 0), s2, NEG)
        s_new.append(s2)
        m = jnp.maximum(m, s2)
    p = jnp.exp(s - m)
    l = jnp.sum(p, axis=1, keepdims=True)
    acc = lax.dot_general(p.astype(BF16), vt, (((1,), (1,)), ((), ())),
                          preferred_element_type=F32)
    for t2 in range(dec):
        pn = jnp.exp(s_new[t2] - m)
        l = l + pn
        acc = acc + pn * vnr[t2:t2 + 1]
    o = acc / l
    lse = jnp.broadcast_to((m + jnp.log(l)) * LOG2_E, o.shape)
    o_rows, l_rows = [], []
    for t in range(dec):
        blk = slice(t * SAMPLE_ROWS, (t + 1) * SAMPLE_ROWS)
        o_rows.append(jnp.sum(jnp.where(diag, o[blk], 0.0), axis=0, keepdims=True))
        l_rows.append(jnp.sum(jnp.where(diag, lse[blk], 0.0), axis=0, keepdims=True))
    pad = jnp.zeros((SAMPLE_ROWS - dec, width), F32)
    o_ref[...] = jnp.concatenate(o_rows + [pad], axis=0)
    l_ref[...] = jnp.concatenate(l_rows + [pad], axis=0)


def _attn_sample_kernel(*refs, dils, dec):
    n_g = len(dils)
    for g, dil in enumerate(dils):
        new_ref, c_ref = refs[g], refs[n_g + g]
        o_ref, l_ref = refs[2 * n_g + 2 * g:2 * n_g + 2 * g + 2]
        q, kn, vn = (new_ref[:, k * GROUP_COLS:(k + 1) * GROUP_COLS] for k in range(3))
        _sample_attn(q, kn, vn, c_ref[0], c_ref[1], o_ref, l_ref, dil, dec)


def _attn_sample_call(new_rows, windows, seq_base, dils, dec, name):
    seqs = windows[0].shape[0]
    in_specs = [pl.BlockSpec((SAMPLE_ROWS, 3 * GROUP_COLS), lambda s: (seq_base + s, 0))
                for _ in new_rows]
    in_specs += [pl.BlockSpec((None, 2, GROUP_COLS, w.shape[-1]), lambda s: (s, 0, 0, 0))
                 for w in windows]
    out_row = pl.BlockSpec((SAMPLE_ROWS, GROUP_COLS), lambda s: (s, 0))
    return pl.pallas_call(
        functools.partial(_attn_sample_kernel, dils=dils, dec=dec),
        grid=(seqs,),
        in_specs=in_specs,
        out_specs=[out_row] * (2 * len(dils)),
        out_shape=[jax.ShapeDtypeStruct((seqs * SAMPLE_ROWS, GROUP_COLS), F32)] * (2 * len(dils)),
        compiler_params=_params(1),
        name=name,
    )(*new_rows, *windows)


def kernel(x_prompt, x_sample, cache_kv1, cache_kv2, cache_kv3, state_conv, ffn1_norm, ffn1_w_in,
           ffn1_w_out, mix_norm, w_in, b_gate, conv_w, w_attn_out, w_conv_out, w_out, ffn2_norm,
           ffn2_w_in, ffn2_w_out, final_norm):
    batch, seq, _ = x_prompt.shape
    n_seq, dec, _ = x_sample.shape
    depth = w_in.shape[0]
    caches = (cache_kv1, cache_kv2, cache_kv3)
    assert dec <= SAMPLE_ROWS - 2 and seq % (GROUPS[-1][1] * QBLOCK * ATTN_BLOCKS) == 0
    for cache, (win, dil) in zip(caches, GROUPS):
        assert cache.shape[2] == win and win == SPAN * dil and (dil == 1 or dec <= dil)

    f1_wi, f1_wo = ffn1_w_in.astype(BF16), ffn1_w_out.astype(BF16)
    f2_wi, f2_wo = ffn2_w_in.astype(BF16), ffn2_w_out.astype(BF16)
    w_qkv, w_conv, w_gate = _cast_split_call(w_in, (QKV_DIM, 3 * CONV_DIM, 2 * D_MODEL), LANES,
                                             "cast_w_in")
    w_ao = w_attn_out.astype(BF16)
    w_co = w_conv_out.astype(BF16)
    w_o = w_out.astype(BF16)
    g1 = ffn1_norm.reshape(depth, 1, D_MODEL)
    gm = mix_norm.reshape(depth, 1, D_MODEL)
    g2 = ffn2_norm.reshape(depth, 1, D_MODEL)
    gf = final_norm.reshape(1, D_MODEL)
    bg = b_gate.reshape(depth, 1, 2 * D_MODEL)

    caches_t = [c.transpose(0, 1, 3, 4, 5, 2).reshape(depth, n_seq, 2, GROUP_COLS, c.shape[2])
                for c in caches]

    xp = x_prompt.reshape(batch * seq, D_MODEL)
    xs = jnp.pad(x_sample, ((0, 0), (0, SAMPLE_ROWS - dec), (0, 0))).reshape(n_seq * SAMPLE_ROWS, D_MODEL)
    ns = xs.shape[0]
    tail_rows = GROUPS[-1][0]
    dils = tuple(dil for _, dil in GROUPS)

    kv_prompt = [[] for _ in GROUPS]
    conv_prompt, conv_sample = [], []
    new_caches = None
    for l in range(depth):
        last = l == depth - 1
        xs = _ffn_call(xs, g1, f1_wi, f1_wo, gf, l, ns, False, f"s_ffn1_{l}")[0]
        s_qkv = _qkv_call(xs, gm, w_qkv, l, ns, ns, ns, (1,) * N_GROUPS, f"s_qkv_{l}")[N_GROUPS:]
        s_tail = [a.reshape(ns, 3 * GROUP_COLS) for a in s_qkv]

        outs = _ffn_call(xp, g1, f1_wi, f1_wo, gf, l, ROW_TILE, False, f"p_ffn1_{l}",
                         roll=(s_tail, caches_t, new_caches, 0, dec))
        xp, new_caches, windows_a = outs[0], outs[1:1 + N_GROUPS], outs[1 + N_GROUPS:]
        p_qkv = _qkv_call(xp, gm, w_qkv, l, ROW_TILE, seq, tail_rows, dils, f"p_qkv_{l}")
        p_attn = []
        for g, (win, dil) in enumerate(GROUPS):
            p_attn.append(_attn_prompt_call(p_qkv[g], f"p_attn{g}_{l}"))
            keep = min(win, seq)
            tail = p_qkv[N_GROUPS + g].transpose(0, 2, 1, 3).reshape(batch, tail_rows, 3 * GROUP_COLS)
            k_t = tail[:, tail_rows - keep:, GROUP_COLS:2 * GROUP_COLS]
            v_t = tail[:, tail_rows - keep:, 2 * GROUP_COLS:]
            kv_prompt[g].append(jnp.stack([k_t, v_t], axis=2).reshape(batch, keep, 2, HEADS, HEAD_DIM))
        xp, p_p = _mixer_call(xp, gm, w_conv, w_gate, bg, conv_w, w_co, w_ao, w_o, None, p_attn, l,
                              ROW_TILE, seq, f"p_mix_{l}")
        conv_prompt.append(p_p.reshape(batch, 8, CONV_DIM)[:, 6:8])
        outs = _ffn_call(xp, g2, f2_wi, f2_wo, gf, l, ROW_TILE, last, f"p_ffn2_{l}",
                         roll=(s_tail, caches_t, new_caches, n_seq // 2, dec))
        xp, new_caches, windows_b = outs[0], outs[1:1 + N_GROUPS], outs[1 + N_GROUPS:]

        attn_a = _attn_sample_call(s_tail, windows_a, 0, dils, dec, f"s_attn_a_{l}")
        attn_b = _attn_sample_call(s_tail, windows_b, n_seq // 2, dils, dec, f"s_attn_b_{l}")
        s_attn = [tuple(jnp.concatenate([attn_a[2 * g + j], attn_b[2 * g + j]], axis=0)
                        .reshape(1, 1, ns, GROUP_COLS) for j in range(2)) for g in range(N_GROUPS)]
        st = jnp.pad(state_conv[l], ((0, 0), (SAMPLE_ROWS - 2, 0), (0, 0)))
        st = jnp.roll(st, -1, axis=0).reshape(ns, CONV_DIM)
        xs, s_p = _mixer_call(xs, gm, w_conv, w_gate, bg, conv_w, w_co, w_ao, w_o, st, s_attn, l,
                              ns, ns, f"s_mix_{l}")
        conv_sample.append(s_p.reshape(n_seq, SAMPLE_ROWS, CONV_DIM)[:, dec - 2:dec])
        xs = _ffn_call(xs, g2, f2_wi, f2_wo, gf, l, ns, last, f"s_ffn2_{l}")[0]

    y_prompt = xp.reshape(batch, seq, D_MODEL)
    y_sample = xs.reshape(n_seq, SAMPLE_ROWS, D_MODEL)[:, :dec]
    kv_sample = [c.reshape(depth, n_seq, 2, HEADS, HEAD_DIM, c.shape[-1]).transpose(0, 1, 5, 2, 3, 4)
                 for c in new_caches]
    return (y_prompt, y_sample,
            jnp.stack(kv_prompt[0]), jnp.stack(kv_prompt[1]), jnp.stack(kv_prompt[2]),
            jnp.stack(conv_prompt),
            kv_sample[0], kv_sample[1], kv_sample[2],
            jnp.stack(conv_sample))
```

```python
import functools

import jax
import jax.numpy as jnp
from jax import lax
from jax.experimental import pallas as pl
from jax.experimental.pallas import tpu as pltpu

F32 = jnp.float32
BF16 = jnp.bfloat16

D_MODEL = 1024
D_FF = 2816
HEAD_DIM = 64
HEADS = 8
GROUP_COLS = HEADS * HEAD_DIM
HALF_COLS = GROUP_COLS // 2
GROUPS = ((128, 1), (512, 4), (2048, 16))
N_GROUPS = len(GROUPS)
SPAN = 128
ATTN_DIM = N_GROUPS * GROUP_COLS
QKV_DIM = 3 * ATTN_DIM
CONV_DIM = D_MODEL
RMS_EPS = 1e-6
ATTN_SCALE = HEAD_DIM ** -0.5
LOG2_E = 1.4426950408889634
LN_2 = 0.6931471805599453
NEG = -1e30

FF_CHUNK = 256
N_FF_CHUNKS = D_FF // FF_CHUNK
WINDOW_AT_CHUNK = {1: 2, 5: 1, 8: 0}
QBLOCK = 128
MAX_ATTN_BLOCKS = 16
LANES = 128
LANE_TILES = GROUP_COLS // LANES
SAMPLE_ROWS = 8
V7X_VMEM_LIMIT_BYTES = 56 * 1024 * 1024
ROW_TILE = 512


def _params(n_axes):
    return pltpu.CompilerParams(
        dimension_semantics=("arbitrary",) * n_axes,
        vmem_limit_bytes=V7X_VMEM_LIMIT_BYTES)


def _resident(block_shape, index_map):
    return pl.BlockSpec(block_shape, index_map, pipeline_mode=pl.Buffered(1))


def _rms_inv(x):
    return lax.rsqrt(jnp.mean(x * x, axis=-1, keepdims=True) + RMS_EPS)


def _cast_split_kernel(w_ref, *out_refs):
    off = 0
    for out in out_refs:
        width = out.shape[-1]
        out[...] = w_ref[:, off:off + width].astype(BF16)
        off += width


def _cast_split_call(w, widths, rows, name):
    depth, r, c = w.shape
    assert sum(widths) == c and r % rows == 0
    return pl.pallas_call(
        _cast_split_kernel,
        grid=(depth, r // rows),
        in_specs=[pl.BlockSpec((None, rows, c), lambda l, i: (l, i, 0))],
        out_specs=[pl.BlockSpec((None, rows, wd), lambda l, i: (l, i, 0)) for wd in widths],
        out_shape=[jax.ShapeDtypeStruct((depth, r, wd), BF16) for wd in widths],
        compiler_params=_params(2),
        name=name,
    )(w)


def _roll_window(c_ref, new_rows, out_ref, dec):
    lb = c_ref.shape[1]
    rolled = pltpu.roll(c_ref[...], lb - dec, axis=1)
    out_ref[...] = rolled
    sub = lax.broadcasted_iota(jnp.int32, new_rows.shape, 0)
    shifted = jnp.where(sub >= SAMPLE_ROWS - dec,
                        pltpu.roll(new_rows, SAMPLE_ROWS - dec, axis=0), 0.0)
    padded = jnp.concatenate(
        [jnp.zeros((LANES - SAMPLE_ROWS, new_rows.shape[1]), F32), shifted], axis=0)
    new_t = padded.T
    lane_t = lax.broadcasted_iota(jnp.int32, new_t.shape, 1)
    out_ref[:, lb - LANES:] = jnp.where(lane_t >= LANES - dec, new_t, rolled[:, lb - LANES:])


def _ffn_kernel(*refs, final, dec, n_alias):
    x_ref, g_ref, wi_ref, wo_ref, fg_ref = refs[:5]
    refs = refs[5:]
    if dec:
        new_refs, cache_refs = refs[:N_GROUPS], refs[N_GROUPS:2 * N_GROUPS]
        refs = refs[2 * N_GROUPS + n_alias:]
        o_ref, roll_refs, refs = refs[0], refs[1:1 + N_GROUPS], refs[1 + N_GROUPS:]
        half_refs, refs = refs[:N_GROUPS], refs[N_GROUPS:]
    else:
        o_ref, refs = refs[0], refs[1:]
    h_s, acc_s = refs
    if dec:
        is_hi = (lax.broadcasted_iota(jnp.int32, (SAMPLE_ROWS, HALF_COLS), 0) * 0
                 + pl.program_id(0) % 2) == 1

        def stream_window(g):
            kn, vn = (jnp.where(is_hi, new_refs[g][:, k * GROUP_COLS + HALF_COLS:(k + 1) * GROUP_COLS],
                                new_refs[g][:, k * GROUP_COLS:k * GROUP_COLS + HALF_COLS])
                      for k in (1, 2))
            _roll_window(cache_refs[g].at[0], kn, roll_refs[g].at[0], dec)
            _roll_window(cache_refs[g].at[1], vn, roll_refs[g].at[1], dec)
            half_refs[g][...] = cache_refs[g][...].astype(BF16)

    x = x_ref[...]
    h_s[...] = ((x * _rms_inv(x)) * g_ref[...]).astype(BF16)
    for c in range(N_FF_CHUNKS):
        if dec and c in WINDOW_AT_CHUNK:
            stream_window(WINDOW_AT_CHUNK[c])
        lo = c * FF_CHUNK
        gate = jnp.dot(h_s[...], wi_ref[:, lo:lo + FF_CHUNK], preferred_element_type=F32)
        up = jnp.dot(h_s[...], wi_ref[:, D_FF + lo:D_FF + lo + FF_CHUNK],
                     preferred_element_type=F32)
        act = ((gate * jax.nn.sigmoid(gate)) * up).astype(BF16)
        part = jnp.dot(act, wo_ref[lo:lo + FF_CHUNK, :], preferred_element_type=F32)
        if c == 0:
            acc_s[...] = part
        else:
            acc_s[...] += part
    y = x_ref[...] + 0.5 * acc_s[...]
    if final:
        y = (y * _rms_inv(y)) * fg_ref[...]
    o_ref[...] = y


def _ffn_call(x, gains, wi, wo, final_gain, layer, tm, final, name, roll=None):
    n = x.shape[0]
    in_specs = [
        pl.BlockSpec((tm, D_MODEL), lambda i: (i, 0)),
        pl.BlockSpec((None, 1, D_MODEL), lambda i: (layer, 0, 0)),
        _resident((None, D_MODEL, 2 * D_FF), lambda i: (layer, 0, 0)),
        _resident((None, D_FF, D_MODEL), lambda i: (layer, 0, 0)),
        pl.BlockSpec((1, D_MODEL), lambda i: (0, 0)),
    ]
    args = [x, gains, wi, wo, final_gain]
    out_specs = [pl.BlockSpec((tm, D_MODEL), lambda i: (i, 0))]
    out_shape = [jax.ShapeDtypeStruct((n, D_MODEL), F32)]
    aliases, dec, n_alias = {}, 0, 0
    scratch = [pltpu.VMEM((tm, D_MODEL), BF16), pltpu.VMEM((tm, D_MODEL), F32)]
    if roll is not None:
        new_rows, caches, prev_outs, seq_base, dec = roll
        assert 2 * (caches[0].shape[1] // 2) == n // tm
        in_specs += [pl.BlockSpec((SAMPLE_ROWS, 3 * GROUP_COLS), lambda i: (seq_base + i // 2, 0))
                     for _ in new_rows]
        args += list(new_rows)
        slab_specs = [pl.BlockSpec((None, None, 2, HALF_COLS, c.shape[-1]),
                                   lambda i: (layer, seq_base + i // 2, 0, i % 2, 0)) for c in caches]
        in_specs += slab_specs
        args += list(caches)
        if prev_outs is not None:
            n_alias = N_GROUPS
            for g, prev in enumerate(prev_outs):
                aliases[len(args)] = 1 + g
                in_specs.append(pl.BlockSpec(memory_space=pl.ANY))
                args.append(prev)
        out_specs += slab_specs
        out_shape += [jax.ShapeDtypeStruct(c.shape, F32) for c in caches]
        out_specs += [pl.BlockSpec((None, 2, HALF_COLS, c.shape[-1]), lambda i: (i // 2, 0, i % 2, 0))
                      for c in caches]
        out_shape += [jax.ShapeDtypeStruct((n // tm // 2, 2, GROUP_COLS, c.shape[-1]), BF16)
                      for c in caches]
    return pl.pallas_call(
        functools.partial(_ffn_kernel, final=final, dec=dec, n_alias=n_alias),
        grid=(n // tm,),
        in_specs=in_specs,
        out_specs=out_specs,
        out_shape=out_shape,
        input_output_aliases=aliases,
        scratch_shapes=scratch,
        compiler_params=_params(1),
        name=name,
    )(*args)


def _qkv_kernel(x_ref, g_ref, w_ref, *refs, dils):
    n_g = len(dils)
    bf_refs, f32_refs = refs[:n_g], refs[n_g:2 * n_g]
    u_s, slab_s = refs[2 * n_g], refs[2 * n_g + 1]
    perm_refs = list(refs[2 * n_g + 2:])
    tm = x_ref.shape[0]
    x = x_ref[...]
    u = (x * _rms_inv(x)) * g_ref[...]
    u_s[...] = u.astype(BF16)
    if perm_refs:
        for t in range(D_MODEL // LANES):
            slab_s[t] = u[:, t * LANES:(t + 1) * LANES]
    for g, dil in enumerate(dils):
        n = tm // dil
        lhs_ref = u_s
        if dil > 1:
            lhs_ref = perm_refs.pop()
            for r in range(dil):
                for t in range(D_MODEL // LANES):
                    lhs_ref[r * n:(r + 1) * n, t * LANES:(t + 1) * LANES] = (
                        slab_s[t, pl.ds(r, n, stride=dil), :].astype(BF16))
        even = (lax.broadcasted_iota(jnp.int32, (n, GROUP_COLS), 1) & (2 * HEAD_DIM - 1)) < HEAD_DIM
        for kind in range(3):
            col = (kind * n_g + g) * GROUP_COLS
            z = jnp.dot(lhs_ref[...], w_ref[:, col:col + GROUP_COLS], preferred_element_type=F32)
            if kind == 0:
                z = z * ATTN_SCALE
            osl = slice(kind * GROUP_COLS, (kind + 1) * GROUP_COLS)
            for r in range(dil):
                zr = z[r * n:(r + 1) * n]
                f32_refs[g][r, :, osl] = zr
                if kind == 0:
                    zq = zr * LOG2_E
                    bf_refs[g][r, :, :GROUP_COLS] = jnp.where(even, zq, 0.0).astype(BF16)
                    bf_refs[g][r, :, GROUP_COLS:2 * GROUP_COLS] = jnp.where(even, 0.0, zq).astype(BF16)
                else:
                    bf_refs[g][r, :, (kind + 1) * GROUP_COLS:(kind + 2) * GROUP_COLS] = zr.astype(BF16)


def _qkv_call(x, gains, w, layer, tm, seq_rows, tail_rows, dils, name):
    n = x.shape[0]
    tiles_per_seq = seq_rows // tm
    tail_tiles = tail_rows // tm
    n_seq = n // seq_rows

    def tail_map(i):
        return (i // tiles_per_seq, 0,
                jnp.maximum(i % tiles_per_seq - (tiles_per_seq - tail_tiles), 0), 0)

    block = lambda dil, parts=3: (None, dil, tm // dil, parts * GROUP_COLS)
    specs = [pl.BlockSpec(block(dil, 4), lambda i: (i // tiles_per_seq, 0, i % tiles_per_seq, 0))
             for dil in dils]
    specs += [pl.BlockSpec(block(dil), tail_map) for dil in dils]
    shapes = [jax.ShapeDtypeStruct((n_seq, dil, seq_rows // dil, 4 * GROUP_COLS), BF16) for dil in dils]
    shapes += [jax.ShapeDtypeStruct((n_seq, dil, tail_rows // dil, 3 * GROUP_COLS), F32) for dil in dils]
    n_perm = sum(dil > 1 for dil in dils)
    return pl.pallas_call(
        functools.partial(_qkv_kernel, dils=dils),
        grid=(n // tm,),
        in_specs=[
            pl.BlockSpec((tm, D_MODEL), lambda i: (i, 0)),
            pl.BlockSpec((None, 1, D_MODEL), lambda i: (layer, 0, 0)),
            _resident((None, D_MODEL, QKV_DIM), lambda i: (layer, 0, 0)),
        ],
        out_specs=specs,
        out_shape=shapes,
        scratch_shapes=[pltpu.VMEM((tm, D_MODEL), BF16),
                        pltpu.VMEM((D_MODEL // LANES, tm, LANES), F32)]
                       + [pltpu.VMEM((tm, D_MODEL), BF16)] * n_perm,
        compiler_params=_params(1),
        name=name,
    )(x, gains, w)


CONV_CHUNK = 256
GATE_CHUNK = 512


def _mixer_kernel(*refs, sample, tiles_per_seq):
    x_ref, g_ref, wc_ref, wg_ref, bg_ref, cw_ref, wco_ref, wao_ref, wo_ref = refs[:9]
    refs = refs[9:]
    if sample:
        st_ref, refs = refs[0], refs[1:]
    attn_refs, refs = refs[:2 * N_GROUPS], refs[2 * N_GROUPS:]
    out_ref, p_ref, u_s, c_s, m_s = refs[:5]
    scratch = list(refs[5:])
    if not sample:
        halo_s = scratch.pop(0)
    tm = x_ref.shape[0]
    x = x_ref[...]
    u_s[...] = ((x * _rms_inv(x)) * g_ref[...]).astype(BF16)
    row = lax.broadcasted_iota(jnp.int32, (tm, CONV_CHUNK), 0)

    if not sample:
        @pl.when(pl.program_id(0) % tiles_per_seq == 0)
        def _():
            halo_s[...] = jnp.zeros_like(halo_s)

    for c in range(CONV_DIM // CONV_CHUNK):
        sl = slice(c * CONV_CHUNK, (c + 1) * CONV_CHUNK)
        u = u_s[...]
        cb = jnp.dot(u, wc_ref[:, sl], preferred_element_type=F32)
        cc = jnp.dot(u, wc_ref[:, CONV_DIM + c * CONV_CHUNK:CONV_DIM + (c + 1) * CONV_CHUNK],
                     preferred_element_type=F32)
        ch = jnp.dot(u, wc_ref[:, 2 * CONV_DIM + c * CONV_CHUNK:2 * CONV_DIM + (c + 1) * CONV_CHUNK],
                     preferred_element_type=F32)
        p = cc * ch
        if sample:
            p_ref[:, sl] = p
            p = jnp.where((row & (SAMPLE_ROWS - 1)) >= SAMPLE_ROWS - 2, st_ref[:, sl], p)
            r1 = pltpu.roll(p, 1, axis=0)
            r2 = pltpu.roll(p, 2, axis=0)
        else:
            r1 = pltpu.roll(p, 1, axis=0)
            r2 = pltpu.roll(p, 2, axis=0)
            h = halo_s[:, sl]
            r1 = jnp.where(row == 0, h[7:8], r1)
            r2 = jnp.where(row == 0, h[6:7], jnp.where(row == 1, h[7:8], r2))
            halo_s[:, sl] = p[tm - 8:tm]
            p_ref[:, sl] = p[tm - 8:tm]
        conv = cw_ref[0:1, sl] * r2 + cw_ref[1:2, sl] * r1 + cw_ref[2:3, sl] * p
        c_s[:, sl] = (cb * conv).astype(BF16)

    def natural(ref):
        dil = ref.shape[0]
        if dil == 1:
            return ref[0]
        s_ref = scratch.pop()
        for r in range(dil):
            for t in range(LANE_TILES):
                s_ref[t, pl.ds(r, tm // dil, stride=dil), :] = ref[r, :, t * LANES:(t + 1) * LANES]
        return jnp.concatenate([s_ref[t] for t in range(LANE_TILES)], axis=1)

    o = [natural(attn_refs[2 * g]) for g in range(N_GROUPS)]
    lse = [natural(attn_refs[2 * g + 1]) for g in range(N_GROUPS)]
    mx = jnp.maximum(jnp.maximum(lse[0], lse[1]), lse[2])
    e = [jnp.exp2(l - mx) for l in lse]
    a = (e[0] * o[0] + e[1] * o[1] + e[2] * o[2]) / (e[0] + e[1] + e[2])
    aproj = jnp.dot(a.astype(BF16), wao_ref[...], preferred_element_type=F32)
    cproj = jnp.dot(c_s[...], wco_ref[...], preferred_element_type=F32)
    for c in range(D_MODEL // GATE_CHUNK):
        sl = slice(c * GATE_CHUNK, (c + 1) * GATE_CHUNK)
        za = jnp.dot(u_s[...], wg_ref[:, sl], preferred_element_type=F32) + bg_ref[:, sl]
        zc = (jnp.dot(u_s[...], wg_ref[:, D_MODEL + c * GATE_CHUNK:D_MODEL + (c + 1) * GATE_CHUNK],
                      preferred_element_type=F32)
              + bg_ref[:, D_MODEL + c * GATE_CHUNK:D_MODEL + (c + 1) * GATE_CHUNK])
        m = jax.nn.sigmoid(za) * aproj[:, sl] + jax.nn.sigmoid(zc) * cproj[:, sl]
        m_s[:, sl] = m.astype(BF16)
    out_ref[...] = x_ref[...] + jnp.dot(m_s[...], wo_ref[...], preferred_element_type=F32)


def _mixer_call(x, gains, wc, wg, bg, cw, wco, wao, wo, state, attn, layer, tm, seq_rows, name):
    n = x.shape[0]
    sample = state is not None
    tiles_per_seq = seq_rows // tm
    row_spec = pl.BlockSpec((tm, D_MODEL), lambda i: (i, 0))
    in_specs = [
        row_spec,
        pl.BlockSpec((None, 1, D_MODEL), lambda i: (layer, 0, 0)),
        _resident((None, D_MODEL, 3 * CONV_DIM), lambda i: (layer, 0, 0)),
        _resident((None, D_MODEL, 2 * D_MODEL), lambda i: (layer, 0, 0)),
        pl.BlockSpec((None, 1, 2 * D_MODEL), lambda i: (layer, 0, 0)),
        pl.BlockSpec((None, 3, CONV_DIM), lambda i: (layer, 0, 0)),
        _resident((None, CONV_DIM, D_MODEL), lambda i: (layer, 0, 0)),
        _resident((None, GROUP_COLS, D_MODEL), lambda i: (layer, 0, 0)),
        _resident((None, D_MODEL, D_MODEL), lambda i: (layer, 0, 0)),
    ]
    args = [x, gains, wc, wg, bg, cw, wco, wao, wo]
    scratch = [pltpu.VMEM((tm, D_MODEL), BF16), pltpu.VMEM((tm, CONV_DIM), BF16),
               pltpu.VMEM((tm, D_MODEL), BF16)]
    if sample:
        in_specs.append(row_spec)
        args.append(state)
        p_spec = row_spec
        p_shape = jax.ShapeDtypeStruct((n, CONV_DIM), F32)
    else:
        p_spec = pl.BlockSpec((8, CONV_DIM), lambda i: (i // tiles_per_seq, 0))
        p_shape = jax.ShapeDtypeStruct((n // seq_rows * 8, CONV_DIM), F32)
        scratch.append(pltpu.VMEM((8, CONV_DIM), F32))
    flat = [a for pair in attn for a in pair]
    in_specs += [pl.BlockSpec((None, a.shape[1], tm // a.shape[1], GROUP_COLS),
                              lambda i: (i // tiles_per_seq, 0, i % tiles_per_seq, 0))
                 for a in flat]
    scratch += [pltpu.VMEM((LANE_TILES, tm, LANES), F32)] * sum(a.shape[1] > 1 for a in flat)
    return pl.pallas_call(
        functools.partial(_mixer_kernel, sample=sample, tiles_per_seq=tiles_per_seq),
        grid=(n // tm,),
        in_specs=in_specs,
        out_specs=[row_spec, p_spec],
        out_shape=[jax.ShapeDtypeStruct((n, D_MODEL), F32), p_shape],
        scratch_shapes=scratch,
        compiler_params=_params(1),
        name=name,
    )(*args, *flat)


def _attn_prompt_kernel(qe_ref, qo_ref, kp_ref, kc_ref, vp_ref, vc_ref, o_ref, l_ref):
    n = pl.program_id(2)
    qi = lax.broadcasted_iota(jnp.int32, (QBLOCK, 2 * QBLOCK), 0)
    col = lax.broadcasted_iota(jnp.int32, (QBLOCK, 2 * QBLOCK), 1)
    cur_ok = (col >= QBLOCK) & (col - QBLOCK <= qi)
    bias_inner = jnp.where(cur_ok | ((col < QBLOCK) & (col >= qi)), 0.0, NEG)
    bias_first = jnp.where(cur_ok | ((col < QBLOCK) & (col >= qi + jnp.where(n > 0, 0, QBLOCK))),
                           0.0, NEG)
    even = lax.broadcasted_iota(jnp.int32, (QBLOCK, 2 * HEAD_DIM), 1) < HEAD_DIM
    streams, step_rows, _ = qe_ref.shape
    for st, i in [(st, i) for st in range(streams) for i in range(step_rows // QBLOCK)]:
        rows = slice(i * QBLOCK, (i + 1) * QBLOCK)
        if i == 0:
            k = jnp.concatenate([kp_ref[st], kc_ref[st, rows]], axis=0)
            v = jnp.concatenate([vp_ref[st], vc_ref[st, rows]], axis=0)
            bias = bias_first
        else:
            k = kc_ref[st, (i - 1) * QBLOCK:(i + 1) * QBLOCK]
            v = vc_ref[st, (i - 1) * QBLOCK:(i + 1) * QBLOCK]
            bias = bias_inner
        for j in range(HEADS // 2):
            sl = slice(2 * HEAD_DIM * j, 2 * HEAD_DIM * (j + 1))
            halves = []
            for q_ref in (qe_ref, qo_ref):
                s = lax.dot_general(q_ref[st, rows, sl], k[:, sl], (((1,), (1,)), ((), ())),
                                    preferred_element_type=F32) + bias
                m = jnp.max(s, axis=1, keepdims=True)
                p = jnp.exp2(s - m)
                l = jnp.sum(p, axis=1, keepdims=True)
                pv = jnp.dot(p.astype(BF16), v[:, sl], preferred_element_type=F32)
                halves.append((pv / l, jnp.broadcast_to(m + jnp.log2(l), pv.shape)))
            o_ref[st, rows, sl] = jnp.where(even, halves[0][0], halves[1][0])
            l_ref[st, rows, sl] = jnp.where(even, halves[0][1], halves[1][1])


def _attn_prompt_call(qkv, name):
    batch, dil, length, _ = qkv.shape
    blocks = min(MAX_ATTN_BLOCKS, length // QBLOCK)
    streams = min(dil, MAX_ATTN_BLOCKS // blocks)
    step_rows = blocks * QBLOCK
    nb = length // step_rows
    assert length % step_rows == 0 and dil % streams == 0

    def spec(part, prev=False):
        if prev:
            return pl.BlockSpec((None, streams, QBLOCK, GROUP_COLS),
                                lambda b, r, n: (b, r, jnp.maximum(blocks * n - 1, 0), part))
        return pl.BlockSpec((None, streams, step_rows, GROUP_COLS), lambda b, r, n: (b, r, n, part))

    out_spec = pl.BlockSpec((None, streams, step_rows, GROUP_COLS), lambda b, r, n: (b, r, n, 0))
    out_shape = jax.ShapeDtypeStruct((batch, dil, length, GROUP_COLS), F32)
    return pl.pallas_call(
        _attn_prompt_kernel,
        grid=(batch, dil // streams, nb),
        in_specs=[spec(0), spec(1), spec(2, True), spec(2), spec(3, True), spec(3)],
        out_specs=[out_spec, out_spec],
        out_shape=[out_shape, out_shape],
        compiler_params=_params(3),
        name=name,
    )(qkv, qkv, qkv, qkv, qkv, qkv)


def _sample_attn(q, kn, vn, kt, vt, o_ref, l_ref, dil, dec):
    width, lb = kt.shape
    sub = lax.broadcasted_iota(jnp.int32, (SAMPLE_ROWS, width), 0)
    lane = lax.broadcasted_iota(jnp.int32, (SAMPLE_ROWS, width), 1)
    diag = sub == lane // HEAD_DIM
    qb = jnp.concatenate(
        [jnp.where(diag, jnp.broadcast_to(q[t:t + 1], diag.shape), 0.0) for t in range(dec)],
        axis=0).astype(BF16)
    rows = dec * SAMPLE_ROWS
    s = jnp.dot(qb, kt, preferred_element_type=F32)
    t_of = lax.broadcasted_iota(jnp.int32, (rows, lb), 0) // SAMPLE_ROWS
    diff = lax.broadcasted_iota(jnp.int32, (rows, lb), 1) - t_of
    valid = (diff >= 0) & ((diff & (dil - 1)) == 0)
    s = jnp.where(valid, s, NEG)
    m = jnp.max(s, axis=1, keepdims=True)

    qbf = qb.astype(F32)
    knr = kn.astype(BF16).astype(F32)
    vnr = vn.astype(BF16).astype(F32)
    t_col = lax.broadcasted_iota(jnp.int32, (rows, 1), 0) // SAMPLE_ROWS
    s_new = []
    for t2 in range(dec):
        s2 = jnp.sum(qbf * knr[t2:t2 + 1], axis=1, keepdims=True)
        d2 = t_col - t2
        s2 = jnp.where((d2 >= 0) & ((d2 & (dil - 1)) == 0), s2, NEG)
        s_new.append(s2)
        m = jnp.maximum(m, s2)
    p = jnp.exp(s - m)
    l = jnp.sum(p, axis=1, keepdims=True)
    acc = lax.dot_general(p.astype(BF16), vt, (((1,), (1,)), ((), ())),
                          preferred_element_type=F32)
    for t2 in range(dec):
        pn = jnp.exp(s_new[t2] - m)
        l = l + pn
        acc = acc + pn * vnr[t2:t2 + 1]
    o = acc / l
    lse = jnp.broadcast_to((m + jnp.log(l)) * LOG2_E, o.shape)
    o_rows, l_rows = [], []
    for t in range(dec):
        blk = slice(t * SAMPLE_ROWS, (t + 1) * SAMPLE_ROWS)
        o_rows.append(jnp.sum(jnp.where(diag, o[blk], 0.0), axis=0, keepdims=True))
        l_rows.append(jnp.sum(jnp.where(diag, lse[blk], 0.0), axis=0, keepdims=True))
    pad = jnp.zeros((SAMPLE_ROWS - dec, width), F32)
    o_ref[...] = jnp.concatenate(o_rows + [pad], axis=0)
    l_ref[...] = jnp.concatenate(l_rows + [pad], axis=0)


def _attn_sample_kernel(*refs, dils, dec):
    n_g = len(dils)
    for g, dil in enumerate(dils):
        new_ref, c_ref = refs[g], refs[n_g + g]
        o_ref, l_ref = refs[2 * n_g + 2 * g:2 * n_g + 2 * g + 2]
        q, kn, vn = (new_ref[:, k * GROUP_COLS:(k + 1) * GROUP_COLS] for k in range(3))
        _sample_attn(q, kn, vn, c_ref[0], c_ref[1], o_ref, l_ref, dil, dec)


def _attn_sample_call(new_rows, windows, seq_base, dils, dec, name):
    seqs = windows[0].shape[0]
    in_specs = [pl.BlockSpec((SAMPLE_ROWS, 3 * GROUP_COLS), lambda s: (seq_base + s, 0))
                for _ in new_rows]
    in_specs += [pl.BlockSpec((None, 2, GROUP_COLS, w.shape[-1]), lambda s: (s, 0, 0, 0))
                 for w in windows]
    out_row = pl.BlockSpec((SAMPLE_ROWS, GROUP_COLS), lambda s: (s, 0))
    return pl.pallas_call(
        functools.partial(_attn_sample_kernel, dils=dils, dec=dec),
        grid=(seqs,),
        in_specs=in_specs,
        out_specs=[out_row] * (2 * len(dils)),
        out_shape=[jax.ShapeDtypeStruct((seqs * SAMPLE_ROWS, GROUP_COLS), F32)] * (2 * len(dils)),
        compiler_params=_params(1),
        name=name,
    )(*new_rows, *windows)


def kernel(x_prompt, x_sample, cache_kv1, cache_kv2, cache_kv3, state_conv, ffn1_norm, ffn1_w_in,
           ffn1_w_out, mix_norm, w_in, b_gate, conv_w, w_attn_out, w_conv_out, w_out, ffn2_norm,
           ffn2_w_in, ffn2_w_out, final_norm):
    batch, seq, _ = x_prompt.shape
    n_seq, dec, _ = x_sample.shape
    depth = w_in.shape[0]
    caches = (cache_kv1, cache_kv2, cache_kv3)
    assert dec <= SAMPLE_ROWS - 2 and seq % (GROUPS[-1][1] * QBLOCK) == 0
    for cache, (win, dil) in zip(caches, GROUPS):
        assert cache.shape[2] == win and win == SPAN * dil and (dil == 1 or dec <= dil)

    f1_wi, f1_wo = ffn1_w_in.astype(BF16), ffn1_w_out.astype(BF16)
    f2_wi, f2_wo = ffn2_w_in.astype(BF16), ffn2_w_out.astype(BF16)
    w_qkv, w_conv, w_gate = _cast_split_call(w_in, (QKV_DIM, 3 * CONV_DIM, 2 * D_MODEL), LANES,
                                             "cast_w_in")
    w_ao = w_attn_out.astype(BF16)
    w_co = w_conv_out.astype(BF16)
    w_o = w_out.astype(BF16)
    g1 = ffn1_norm.reshape(depth, 1, D_MODEL)
    gm = mix_norm.reshape(depth, 1, D_MODEL)
    g2 = ffn2_norm.reshape(depth, 1, D_MODEL)
    gf = final_norm.reshape(1, D_MODEL)
    bg = b_gate.reshape(depth, 1, 2 * D_MODEL)

    caches_t = [c.transpose(0, 1, 3, 4, 5, 2).reshape(depth, n_seq, 2, GROUP_COLS, c.shape[2])
                for c in caches]

    xp = x_prompt.reshape(batch * seq, D_MODEL)
    xs = jnp.pad(x_sample, ((0, 0), (0, SAMPLE_ROWS - dec), (0, 0))).reshape(n_seq * SAMPLE_ROWS, D_MODEL)
    ns = xs.shape[0]
    tail_rows = GROUPS[-1][0]
    dils = tuple(dil for _, dil in GROUPS)

    kv_prompt = [[] for _ in GROUPS]
    conv_prompt, conv_sample = [], []
    new_caches = None
    for l in range(depth):
        last = l == depth - 1
        xs = _ffn_call(xs, g1, f1_wi, f1_wo, gf, l, ns, False, f"s_ffn1_{l}")[0]
        s_qkv = _qkv_call(xs, gm, w_qkv, l, ns, ns, ns, (1,) * N_GROUPS, f"s_qkv_{l}")[N_GROUPS:]
        s_tail = [a.reshape(ns, 3 * GROUP_COLS) for a in s_qkv]

        outs = _ffn_call(xp, g1, f1_wi, f1_wo, gf, l, ROW_TILE, False, f"p_ffn1_{l}",
                         roll=(s_tail, caches_t, new_caches, 0, dec))
        xp, new_caches, windows_a = outs[0], outs[1:1 + N_GROUPS], outs[1 + N_GROUPS:]
        p_qkv = _qkv_call(xp, gm, w_qkv, l, ROW_TILE, seq, tail_rows, dils, f"p_qkv_{l}")
        p_attn = []
        for g, (win, dil) in enumerate(GROUPS):
            p_attn.append(_attn_prompt_call(p_qkv[g], f"p_attn{g}_{l}"))
            keep = min(win, seq)
            tail = p_qkv[N_GROUPS + g].transpose(0, 2, 1, 3).reshape(batch, tail_rows, 3 * GROUP_COLS)
            k_t = tail[:, tail_rows - keep:, GROUP_COLS:2 * GROUP_COLS]
            v_t = tail[:, tail_rows - keep:, 2 * GROUP_COLS:]
            kv_prompt[g].append(jnp.stack([k_t, v_t], axis=2).reshape(batch, keep, 2, HEADS, HEAD_DIM))
        xp, p_p = _mixer_call(xp, gm, w_conv, w_gate, bg, conv_w, w_co, w_ao, w_o, None, p_attn, l,
                              ROW_TILE, seq, f"p_mix_{l}")
        conv_prompt.append(p_p.reshape(batch, 8, CONV_DIM)[:, 6:8])
        outs = _ffn_call(xp, g2, f2_wi, f2_wo, gf, l, ROW_TILE, last, f"p_ffn2_{l}",
                         roll=(s_tail, caches_t, new_caches, n_seq // 2, dec))
        xp, new_caches, windows_b = outs[0], outs[1:1 + N_GROUPS], outs[1 + N_GROUPS:]

        attn_a = _attn_sample_call(s_tail, windows_a, 0, dils, dec, f"s_attn_a_{l}")
        attn_b = _attn_sample_call(s_tail, windows_b, n_seq // 2, dils, dec, f"s_attn_b_{l}")
        s_attn = [tuple(jnp.concatenate([attn_a[2 * g + j], attn_b[2 * g + j]], axis=0)
                        .reshape(1, 1, ns, GROUP_COLS) for j in range(2)) for g in range(N_GROUPS)]
        st = jnp.pad(state_conv[l], ((0, 0), (SAMPLE_ROWS - 2, 0), (0, 0)))
        st = jnp.roll(st, -1, axis=0).reshape(ns, CONV_DIM)
        xs, s_p = _mixer_call(xs, gm, w_conv, w_gate, bg, conv_w, w_co, w_ao, w_o, st, s_attn, l,
                              ns, ns, f"s_mix_{l}")
        conv_sample.append(s_p.reshape(n_seq, SAMPLE_ROWS, CONV_DIM)[:, dec - 2:dec])
        xs = _ffn_call(xs, g2, f2_wi, f2_wo, gf, l, ns, last, f"s_ffn2_{l}")[0]

    y_prompt = xp.reshape(batch, seq, D_MODEL)
    y_sample = xs.reshape(n_seq, SAMPLE_ROWS, D_MODEL)[:, :dec]
    kv_sample = [c.reshape(depth, n_seq, 2, HEADS, HEAD_DIM, c.shape[-1]).transpose(0, 1, 5, 2, 3, 4)
                 for c in new_caches]
    return (y_prompt, y_sample,
            jnp.stack(kv_prompt[0]), jnp.stack(kv_prompt[1]), jnp.stack(kv_prompt[2]),
            jnp.stack(conv_prompt),
            kv_sample[0], kv_sample[1], kv_sample[2],
            jnp.stack(conv_sample))
```

```python
import functools

import jax
import jax.numpy as jnp
from jax import lax
from jax.experimental import pallas as pl
from jax.experimental.pallas import tpu as pltpu

F32 = jnp.float32
BF16 = jnp.bfloat16

D_MODEL = 1024
D_FF = 2816
HEAD_DIM = 64
HEADS = 8
GROUP_COLS = HEADS * HEAD_DIM
HALF_COLS = GROUP_COLS // 2
GROUPS = ((128, 1), (512, 4), (2048, 16))
N_GROUPS = len(GROUPS)
SPAN = 128
ATTN_DIM = N_GROUPS * GROUP_COLS
QKV_DIM = 3 * ATTN_DIM
CONV_DIM = D_MODEL
RMS_EPS = 1e-6
ATTN_SCALE = HEAD_DIM ** -0.5
LOG2_E = 1.4426950408889634
LN_2 = 0.6931471805599453
NEG = -1e30

FF_CHUNK = 256
N_FF_CHUNKS = D_FF // FF_CHUNK
QBLOCK = 128
MAX_ATTN_BLOCKS = 16
LANES = 128
LANE_TILES = GROUP_COLS // LANES
SAMPLE_ROWS = 8
SAMPLE_SEQS_PER_STEP = 2
V7X_VMEM_LIMIT_BYTES = 56 * 1024 * 1024
ROW_TILE = 512


def _params(n_axes):
    return pltpu.CompilerParams(
        dimension_semantics=("arbitrary",) * n_axes,
        vmem_limit_bytes=V7X_VMEM_LIMIT_BYTES)


def _resident(block_shape, index_map):
    return pl.BlockSpec(block_shape, index_map, pipeline_mode=pl.Buffered(1))


def _rms_inv(x):
    return lax.rsqrt(jnp.mean(x * x, axis=-1, keepdims=True) + RMS_EPS)


def _cast_split_kernel(w_ref, *out_refs):
    off = 0
    for out in out_refs:
        width = out.shape[-1]
        out[...] = w_ref[:, off:off + width].astype(BF16)
        off += width


def _cast_split_call(w, widths, rows, name):
    depth, r, c = w.shape
    assert sum(widths) == c and r % rows == 0
    return pl.pallas_call(
        _cast_split_kernel,
        grid=(depth, r // rows),
        in_specs=[pl.BlockSpec((None, rows, c), lambda l, i: (l, i, 0))],
        out_specs=[pl.BlockSpec((None, rows, wd), lambda l, i: (l, i, 0)) for wd in widths],
        out_shape=[jax.ShapeDtypeStruct((depth, r, wd), BF16) for wd in widths],
        compiler_params=_params(2),
        name=name,
    )(w)


def _roll_window(c_ref, new_rows, out_ref, dec):
    lb = c_ref.shape[1]
    rolled = pltpu.roll(c_ref[...], lb - dec, axis=1)
    out_ref[...] = rolled
    sub = lax.broadcasted_iota(jnp.int32, new_rows.shape, 0)
    shifted = jnp.where(sub >= SAMPLE_ROWS - dec,
                        pltpu.roll(new_rows, SAMPLE_ROWS - dec, axis=0), 0.0)
    padded = jnp.concatenate(
        [jnp.zeros((LANES - SAMPLE_ROWS, new_rows.shape[1]), F32), shifted], axis=0)
    new_t = padded.T
    lane_t = lax.broadcasted_iota(jnp.int32, new_t.shape, 1)
    out_ref[:, lb - LANES:] = jnp.where(lane_t >= LANES - dec, new_t, rolled[:, lb - LANES:])


def _ffn_kernel(*refs, final, dec, n_alias):
    x_ref, g_ref, wi_ref, wo_ref, fg_ref = refs[:5]
    refs = refs[5:]
    if dec:
        new_refs, cache_refs = refs[:N_GROUPS], refs[N_GROUPS:2 * N_GROUPS]
        refs = refs[2 * N_GROUPS + n_alias:]
        o_ref, roll_refs, refs = refs[0], refs[1:1 + N_GROUPS], refs[1 + N_GROUPS:]
        half_refs, refs = refs[:N_GROUPS], refs[N_GROUPS:]
    else:
        o_ref, refs = refs[0], refs[1:]
    h_s, acc_s = refs
    if dec:
        is_hi = (lax.broadcasted_iota(jnp.int32, (SAMPLE_ROWS, HALF_COLS), 0) * 0
                 + pl.program_id(0) % 2) == 1

        for g in range(N_GROUPS):
            kn, vn = (jnp.where(is_hi, new_refs[g][:, k * GROUP_COLS + HALF_COLS:(k + 1) * GROUP_COLS],
                                new_refs[g][:, k * GROUP_COLS:k * GROUP_COLS + HALF_COLS])
                      for k in (1, 2))
            _roll_window(cache_refs[g].at[0], kn, roll_refs[g].at[0], dec)
            _roll_window(cache_refs[g].at[1], vn, roll_refs[g].at[1], dec)
            half_refs[g][...] = cache_refs[g][...].astype(BF16)
    x = x_ref[...]
    h_s[...] = ((x * _rms_inv(x)) * g_ref[...]).astype(BF16)
    for c in range(N_FF_CHUNKS):
        lo = c * FF_CHUNK
        gate = jnp.dot(h_s[...], wi_ref[:, lo:lo + FF_CHUNK], preferred_element_type=F32)
        up = jnp.dot(h_s[...], wi_ref[:, D_FF + lo:D_FF + lo + FF_CHUNK],
                     preferred_element_type=F32)
        act = ((gate * jax.nn.sigmoid(gate)) * up).astype(BF16)
        part = jnp.dot(act, wo_ref[lo:lo + FF_CHUNK, :], preferred_element_type=F32)
        if c == 0:
            acc_s[...] = part
        else:
            acc_s[...] += part
    y = x_ref[...] + 0.5 * acc_s[...]
    if final:
        y = (y * _rms_inv(y)) * fg_ref[...]
    o_ref[...] = y


def _ffn_call(x, gains, wi, wo, final_gain, layer, tm, final, name, roll=None):
    n = x.shape[0]
    in_specs = [
        pl.BlockSpec((tm, D_MODEL), lambda i: (i, 0)),
        pl.BlockSpec((None, 1, D_MODEL), lambda i: (layer, 0, 0)),
        _resident((None, D_MODEL, 2 * D_FF), lambda i: (layer, 0, 0)),
        _resident((None, D_FF, D_MODEL), lambda i: (layer, 0, 0)),
        pl.BlockSpec((1, D_MODEL), lambda i: (0, 0)),
    ]
    args = [x, gains, wi, wo, final_gain]
    out_specs = [pl.BlockSpec((tm, D_MODEL), lambda i: (i, 0))]
    out_shape = [jax.ShapeDtypeStruct((n, D_MODEL), F32)]
    aliases, dec, n_alias = {}, 0, 0
    scratch = [pltpu.VMEM((tm, D_MODEL), BF16), pltpu.VMEM((tm, D_MODEL), F32)]
    if roll is not None:
        new_rows, caches, prev_outs, seq_base, dec = roll
        assert 2 * (caches[0].shape[1] // 2) == n // tm
        in_specs += [pl.BlockSpec((SAMPLE_ROWS, 3 * GROUP_COLS), lambda i: (seq_base + i // 2, 0))
                     for _ in new_rows]
        args += list(new_rows)
        slab_specs = [pl.BlockSpec((None, None, 2, HALF_COLS, c.shape[-1]),
                                   lambda i: (layer, seq_base + i // 2, 0, i % 2, 0)) for c in caches]
        in_specs += slab_specs
        args += list(caches)
        if prev_outs is not None:
            n_alias = N_GROUPS
            for g, prev in enumerate(prev_outs):
                aliases[len(args)] = 1 + g
                in_specs.append(pl.BlockSpec(memory_space=pl.ANY))
                args.append(prev)
        out_specs += slab_specs
        out_shape += [jax.ShapeDtypeStruct(c.shape, F32) for c in caches]
        out_specs += [pl.BlockSpec((None, 2, HALF_COLS, c.shape[-1]), lambda i: (i // 2, 0, i % 2, 0))
                      for c in caches]
        out_shape += [jax.ShapeDtypeStruct((n // tm // 2, 2, GROUP_COLS, c.shape[-1]), BF16)
                      for c in caches]
    return pl.pallas_call(
        functools.partial(_ffn_kernel, final=final, dec=dec, n_alias=n_alias),
        grid=(n // tm,),
        in_specs=in_specs,
        out_specs=out_specs,
        out_shape=out_shape,
        input_output_aliases=aliases,
        scratch_shapes=scratch,
        compiler_params=_params(1),
        name=name,
    )(*args)


def _qkv_kernel(x_ref, g_ref, w_ref, *refs, dils):
    n_g = len(dils)
    bf_refs, f32_refs = refs[:n_g], refs[n_g:2 * n_g]
    u_s, slab_s = refs[2 * n_g], refs[2 * n_g + 1]
    perm_refs = list(refs[2 * n_g + 2:])
    tm = x_ref.shape[0]
    x = x_ref[...]
    u = (x * _rms_inv(x)) * g_ref[...]
    u_s[...] = u.astype(BF16)
    if perm_refs:
        for t in range(D_MODEL // LANES):
            slab_s[t] = u[:, t * LANES:(t + 1) * LANES]
    for g, dil in enumerate(dils):
        n = tm // dil
        lhs_ref = u_s
        if dil > 1:
            lhs_ref = perm_refs.pop()
            for r in range(dil):
                for t in range(D_MODEL // LANES):
                    lhs_ref[r * n:(r + 1) * n, t * LANES:(t + 1) * LANES] = (
                        slab_s[t, pl.ds(r, n, stride=dil), :].astype(BF16))
        even = (lax.broadcasted_iota(jnp.int32, (n, GROUP_COLS), 1) & (2 * HEAD_DIM - 1)) < HEAD_DIM
        for kind in range(3):
            col = (kind * n_g + g) * GROUP_COLS
            z = jnp.dot(lhs_ref[...], w_ref[:, col:col + GROUP_COLS], preferred_element_type=F32)
            if kind == 0:
                z = z * ATTN_SCALE
            osl = slice(kind * GROUP_COLS, (kind + 1) * GROUP_COLS)
            for r in range(dil):
                zr = z[r * n:(r + 1) * n]
                f32_refs[g][r, :, osl] = zr
                if kind == 0:
                    zq = zr * LOG2_E
                    bf_refs[g][r, :, :GROUP_COLS] = jnp.where(even, zq, 0.0).astype(BF16)
                    bf_refs[g][r, :, GROUP_COLS:2 * GROUP_COLS] = jnp.where(even, 0.0, zq).astype(BF16)
                else:
                    bf_refs[g][r, :, (kind + 1) * GROUP_COLS:(kind + 2) * GROUP_COLS] = zr.astype(BF16)


def _qkv_call(x, gains, w, layer, tm, seq_rows, tail_rows, dils, name):
    n = x.shape[0]
    tiles_per_seq = seq_rows // tm
    tail_tiles = tail_rows // tm
    n_seq = n // seq_rows

    def tail_map(i):
        return (i // tiles_per_seq, 0,
                jnp.maximum(i % tiles_per_seq - (tiles_per_seq - tail_tiles), 0), 0)

    block = lambda dil, parts=3: (None, dil, tm // dil, parts * GROUP_COLS)
    specs = [pl.BlockSpec(block(dil, 4), lambda i: (i // tiles_per_seq, 0, i % tiles_per_seq, 0))
             for dil in dils]
    specs += [pl.BlockSpec(block(dil), tail_map) for dil in dils]
    shapes = [jax.ShapeDtypeStruct((n_seq, dil, seq_rows // dil, 4 * GROUP_COLS), BF16) for dil in dils]
    shapes += [jax.ShapeDtypeStruct((n_seq, dil, tail_rows // dil, 3 * GROUP_COLS), F32) for dil in dils]
    n_perm = sum(dil > 1 for dil in dils)
    return pl.pallas_call(
        functools.partial(_qkv_kernel, dils=dils),
        grid=(n // tm,),
        in_specs=[
            pl.BlockSpec((tm, D_MODEL), lambda i: (i, 0)),
            pl.BlockSpec((None, 1, D_MODEL), lambda i: (layer, 0, 0)),
            _resident((None, D_MODEL, QKV_DIM), lambda i: (layer, 0, 0)),
        ],
        out_specs=specs,
        out_shape=shapes,
        scratch_shapes=[pltpu.VMEM((tm, D_MODEL), BF16),
                        pltpu.VMEM((D_MODEL // LANES, tm, LANES), F32)]
                       + [pltpu.VMEM((tm, D_MODEL), BF16)] * n_perm,
        compiler_params=_params(1),
        name=name,
    )(x, gains, w)


CONV_CHUNK = 256
GATE_CHUNK = 512


def _mixer_kernel(*refs, sample, tiles_per_seq):
    x_ref, g_ref, wc_ref, wg_ref, bg_ref, cw_ref, wco_ref, wao_ref, wo_ref = refs[:9]
    refs = refs[9:]
    if sample:
        st_ref, refs = refs[0], refs[1:]
    attn_refs, refs = refs[:2 * N_GROUPS], refs[2 * N_GROUPS:]
    out_ref, p_ref, u_s, c_s, m_s = refs[:5]
    scratch = list(refs[5:])
    if not sample:
        halo_s = scratch.pop(0)
    tm = x_ref.shape[0]
    x = x_ref[...]
    u_s[...] = ((x * _rms_inv(x)) * g_ref[...]).astype(BF16)
    row = lax.broadcasted_iota(jnp.int32, (tm, CONV_CHUNK), 0)

    if not sample:
        @pl.when(pl.program_id(0) % tiles_per_seq == 0)
        def _():
            halo_s[...] = jnp.zeros_like(halo_s)

    for c in range(CONV_DIM // CONV_CHUNK):
        sl = slice(c * CONV_CHUNK, (c + 1) * CONV_CHUNK)
        u = u_s[...]
        cb = jnp.dot(u, wc_ref[:, sl], preferred_element_type=F32)
        cc = jnp.dot(u, wc_ref[:, CONV_DIM + c * CONV_CHUNK:CONV_DIM + (c + 1) * CONV_CHUNK],
                     preferred_element_type=F32)
        ch = jnp.dot(u, wc_ref[:, 2 * CONV_DIM + c * CONV_CHUNK:2 * CONV_DIM + (c + 1) * CONV_CHUNK],
                     preferred_element_type=F32)
        p = cc * ch
        if sample:
            p_ref[:, sl] = p
            p = jnp.where((row & (SAMPLE_ROWS - 1)) >= SAMPLE_ROWS - 2, st_ref[:, sl], p)
            r1 = pltpu.roll(p, 1, axis=0)
            r2 = pltpu.roll(p, 2, axis=0)
        else:
            r1 = pltpu.roll(p, 1, axis=0)
            r2 = pltpu.roll(p, 2, axis=0)
            h = halo_s[:, sl]
            r1 = jnp.where(row == 0, h[7:8], r1)
            r2 = jnp.where(row == 0, h[6:7], jnp.where(row == 1, h[7:8], r2))
            halo_s[:, sl] = p[tm - 8:tm]
            p_ref[:, sl] = p[tm - 8:tm]
        conv = cw_ref[0:1, sl] * r2 + cw_ref[1:2, sl] * r1 + cw_ref[2:3, sl] * p
        c_s[:, sl] = (cb * conv).astype(BF16)

    def natural(ref):
        dil = ref.shape[0]
        if dil == 1:
            return ref[0]
        s_ref = scratch.pop()
        for r in range(dil):
            for t in range(LANE_TILES):
                s_ref[t, pl.ds(r, tm // dil, stride=dil), :] = ref[r, :, t * LANES:(t + 1) * LANES]
        return jnp.concatenate([s_ref[t] for t in range(LANE_TILES)], axis=1)

    o = [natural(attn_refs[2 * g]) for g in range(N_GROUPS)]
    lse = [natural(attn_refs[2 * g + 1]) for g in range(N_GROUPS)]
    mx = jnp.maximum(jnp.maximum(lse[0], lse[1]), lse[2])
    e = [jnp.exp2(l - mx) for l in lse]
    a = (e[0] * o[0] + e[1] * o[1] + e[2] * o[2]) / (e[0] + e[1] + e[2])
    aproj = jnp.dot(a.astype(BF16), wao_ref[...], preferred_element_type=F32)
    cproj = jnp.dot(c_s[...], wco_ref[...], preferred_element_type=F32)
    for c in range(D_MODEL // GATE_CHUNK):
        sl = slice(c * GATE_CHUNK, (c + 1) * GATE_CHUNK)
        za = jnp.dot(u_s[...], wg_ref[:, sl], preferred_element_type=F32) + bg_ref[:, sl]
        zc = (jnp.dot(u_s[...], wg_ref[:, D_MODEL + c * GATE_CHUNK:D_MODEL + (c + 1) * GATE_CHUNK],
                      preferred_element_type=F32)
              + bg_ref[:, D_MODEL + c * GATE_CHUNK:D_MODEL + (c + 1) * GATE_CHUNK])
        m = jax.nn.sigmoid(za) * aproj[:, sl] + jax.nn.sigmoid(zc) * cproj[:, sl]
        m_s[:, sl] = m.astype(BF16)
    out_ref[...] = x_ref[...] + jnp.dot(m_s[...], wo_ref[...], preferred_element_type=F32)


def _mixer_call(x, gains, wc, wg, bg, cw, wco, wao, wo, state, attn, layer, tm, seq_rows, name):
    n = x.shape[0]
    sample = state is not None
    tiles_per_seq = seq_rows // tm
    row_spec = pl.BlockSpec((tm, D_MODEL), lambda i: (i, 0))
    in_specs = [
        row_spec,
        pl.BlockSpec((None, 1, D_MODEL), lambda i: (layer, 0, 0)),
        _resident((None, D_MODEL, 3 * CONV_DIM), lambda i: (layer, 0, 0)),
        _resident((None, D_MODEL, 2 * D_MODEL), lambda i: (layer, 0, 0)),
        pl.BlockSpec((None, 1, 2 * D_MODEL), lambda i: (layer, 0, 0)),
        pl.BlockSpec((None, 3, CONV_DIM), lambda i: (layer, 0, 0)),
        _resident((None, CONV_DIM, D_MODEL), lambda i: (layer, 0, 0)),
        _resident((None, GROUP_COLS, D_MODEL), lambda i: (layer, 0, 0)),
        _resident((None, D_MODEL, D_MODEL), lambda i: (layer, 0, 0)),
    ]
    args = [x, gains, wc, wg, bg, cw, wco, wao, wo]
    scratch = [pltpu.VMEM((tm, D_MODEL), BF16), pltpu.VMEM((tm, CONV_DIM), BF16),
               pltpu.VMEM((tm, D_MODEL), BF16)]
    if sample:
        in_specs.append(row_spec)
        args.append(state)
        p_spec = row_spec
        p_shape = jax.ShapeDtypeStruct((n, CONV_DIM), F32)
    else:
        p_spec = pl.BlockSpec((8, CONV_DIM), lambda i: (i // tiles_per_seq, 0))
        p_shape = jax.ShapeDtypeStruct((n // seq_rows * 8, CONV_DIM), F32)
        scratch.append(pltpu.VMEM((8, CONV_DIM), F32))
    flat = [a for pair in attn for a in pair]
    in_specs += [pl.BlockSpec((None, a.shape[1], tm // a.shape[1], GROUP_COLS),
                              lambda i: (i // tiles_per_seq, 0, i % tiles_per_seq, 0))
                 for a in flat]
    scratch += [pltpu.VMEM((LANE_TILES, tm, LANES), F32)] * sum(a.shape[1] > 1 for a in flat)
    return pl.pallas_call(
        functools.partial(_mixer_kernel, sample=sample, tiles_per_seq=tiles_per_seq),
        grid=(n // tm,),
        in_specs=in_specs,
        out_specs=[row_spec, p_spec],
        out_shape=[jax.ShapeDtypeStruct((n, D_MODEL), F32), p_shape],
        scratch_shapes=scratch,
        compiler_params=_params(1),
        name=name,
    )(*args, *flat)


def _attn_prompt_kernel(qe_ref, qo_ref, kp_ref, kc_ref, vp_ref, vc_ref, o_ref, l_ref):
    n = pl.program_id(2)
    qi = lax.broadcasted_iota(jnp.int32, (QBLOCK, 2 * QBLOCK), 0)
    col = lax.broadcasted_iota(jnp.int32, (QBLOCK, 2 * QBLOCK), 1)
    cur_ok = (col >= QBLOCK) & (col - QBLOCK <= qi)
    bias_inner = jnp.where(cur_ok | ((col < QBLOCK) & (col >= qi)), 0.0, NEG)
    bias_first = jnp.where(cur_ok | ((col < QBLOCK) & (col >= qi + jnp.where(n > 0, 0, QBLOCK))),
                           0.0, NEG)
    even = lax.broadcasted_iota(jnp.int32, (QBLOCK, 2 * HEAD_DIM), 1) < HEAD_DIM
    streams, step_rows, _ = qe_ref.shape
    for st, i in [(st, i) for st in range(streams) for i in range(step_rows // QBLOCK)]:
        rows = slice(i * QBLOCK, (i + 1) * QBLOCK)
        if i == 0:
            k = jnp.concatenate([kp_ref[st], kc_ref[st, rows]], axis=0)
            v = jnp.concatenate([vp_ref[st], vc_ref[st, rows]], axis=0)
            bias = bias_first
        else:
            k = kc_ref[st, (i - 1) * QBLOCK:(i + 1) * QBLOCK]
            v = vc_ref[st, (i - 1) * QBLOCK:(i + 1) * QBLOCK]
            bias = bias_inner
        for j in range(HEADS // 2):
            sl = slice(2 * HEAD_DIM * j, 2 * HEAD_DIM * (j + 1))
            halves = []
            for q_ref in (qe_ref, qo_ref):
                s = lax.dot_general(q_ref[st, rows, sl], k[:, sl], (((1,), (1,)), ((), ())),
                                    preferred_element_type=F32) + bias
                m = jnp.max(s, axis=1, keepdims=True)
                p = jnp.exp2(s - m)
                l = jnp.sum(p, axis=1, keepdims=True)
                pv = jnp.dot(p.astype(BF16), v[:, sl], preferred_element_type=F32)
                halves.append((pv / l, jnp.broadcast_to(m + jnp.log2(l), pv.shape)))
            o_ref[st, rows, sl] = jnp.where(even, halves[0][0], halves[1][0])
            l_ref[st, rows, sl] = jnp.where(even, halves[0][1], halves[1][1])


def _attn_prompt_call(qkv, name):
    batch, dil, length, _ = qkv.shape
    blocks = min(MAX_ATTN_BLOCKS, length // QBLOCK)
    streams = min(dil, MAX_ATTN_BLOCKS // blocks)
    step_rows = blocks * QBLOCK
    nb = length // step_rows
    assert length % step_rows == 0 and dil % streams == 0

    def spec(part, prev=False):
        if prev:
            return pl.BlockSpec((None, streams, QBLOCK, GROUP_COLS),
                                lambda b, r, n: (b, r, jnp.maximum(blocks * n - 1, 0), part))
        return pl.BlockSpec((None, streams, step_rows, GROUP_COLS), lambda b, r, n: (b, r, n, part))

    out_spec = pl.BlockSpec((None, streams, step_rows, GROUP_COLS), lambda b, r, n: (b, r, n, 0))
    out_shape = jax.ShapeDtypeStruct((batch, dil, length, GROUP_COLS), F32)
    return pl.pallas_call(
        _attn_prompt_kernel,
        grid=(batch, dil // streams, nb),
        in_specs=[spec(0), spec(1), spec(2, True), spec(2), spec(3, True), spec(3)],
        out_specs=[out_spec, out_spec],
        out_shape=[out_shape, out_shape],
        compiler_params=_params(3),
        name=name,
    )(qkv, qkv, qkv, qkv, qkv, qkv)


def _sample_attn(q, kn, vn, kt, vt, o_ref, l_ref, dil, dec):
    width, lb = kt.shape
    sub = lax.broadcasted_iota(jnp.int32, (SAMPLE_ROWS, width), 0)
    lane = lax.broadcasted_iota(jnp.int32, (SAMPLE_ROWS, width), 1)
    diag = sub == lane // HEAD_DIM
    qb = jnp.concatenate(
        [jnp.where(diag, jnp.broadcast_to(q[t:t + 1], diag.shape), 0.0) for t in range(dec)],
        axis=0).astype(BF16)
    rows = dec * SAMPLE_ROWS
    s = jnp.dot(qb, kt, preferred_element_type=F32)
    t_of = lax.broadcasted_iota(jnp.int32, (rows, lb), 0) // SAMPLE_ROWS
    diff = lax.broadcasted_iota(jnp.int32, (rows, lb), 1) - t_of
    valid = (diff >= 0) & ((diff & (dil - 1)) == 0)
    s = jnp.where(valid, s, NEG)
    m = jnp.max(s, axis=1, keepdims=True)

    qbf = qb.astype(F32)
    knr = kn.astype(BF16).astype(F32)
    vnr = vn.astype(BF16).astype(F32)
    t_col = lax.broadcasted_iota(jnp.int32, (rows, 1), 0) // SAMPLE_ROWS
    s_new = []
    for t2 in range(dec):
        s2 = jnp.sum(qbf * knr[t2:t2 + 1], axis=1, keepdims=True)
        d2 = t_col - t2
        s2 = jnp.where((d2 >= 0) & ((d2 & (dil - 1)) == 0), s2, NEG)
        s_new.append(s2)
        m = jnp.maximum(m, s2)
    p = jnp.exp(s - m)
    l = jnp.sum(p, axis=1, keepdims=True)
    acc = lax.dot_general(p.astype(BF16), vt, (((1,), (1,)), ((), ())),
                          preferred_element_type=F32)
    for t2 in range(dec):
        pn = jnp.exp(s_new[t2] - m)
        l = l + pn
        acc = acc + pn * vnr[t2:t2 + 1]
    o = acc / l
    lse = jnp.broadcast_to((m + jnp.log(l)) * LOG2_E, o.shape)
    o_rows, l_rows = [], []
    for t in range(dec):
        blk = slice(t * SAMPLE_ROWS, (t + 1) * SAMPLE_ROWS)
        o_rows.append(jnp.sum(jnp.where(diag, o[blk], 0.0), axis=0, keepdims=True))
        l_rows.append(jnp.sum(jnp.where(diag, lse[blk], 0.0), axis=0, keepdims=True))
    pad = jnp.zeros((SAMPLE_ROWS - dec, width), F32)
    o_ref[...] = jnp.concatenate(o_rows + [pad], axis=0)
    l_ref[...] = jnp.concatenate(l_rows + [pad], axis=0)


def _attn_sample_kernel(*refs, dils, dec):
    n_g = len(dils)
    for j in range(refs[n_g].shape[0]):
        rows = slice(j * SAMPLE_ROWS, (j + 1) * SAMPLE_ROWS)
        for g, dil in enumerate(dils):
            new_ref, c_ref = refs[g], refs[n_g + g]
            o_ref, l_ref = refs[2 * n_g + 2 * g:2 * n_g + 2 * g + 2]
            q, kn, vn = (new_ref[rows, k * GROUP_COLS:(k + 1) * GROUP_COLS] for k in range(3))
            _sample_attn(q, kn, vn, c_ref[j, 0], c_ref[j, 1], o_ref.at[rows], l_ref.at[rows], dil, dec)


def _attn_sample_call(new_rows, windows, seq_base, dils, dec, name):
    seqs = windows[0].shape[0]
    per = SAMPLE_SEQS_PER_STEP
    assert seqs % per == 0 and seq_base % per == 0
    in_specs = [pl.BlockSpec((per * SAMPLE_ROWS, 3 * GROUP_COLS), lambda s: (seq_base // per + s, 0))
                for _ in new_rows]
    in_specs += [pl.BlockSpec((per, 2, GROUP_COLS, w.shape[-1]), lambda s: (s, 0, 0, 0))
                 for w in windows]
    out_row = pl.BlockSpec((per * SAMPLE_ROWS, GROUP_COLS), lambda s: (s, 0))
    return pl.pallas_call(
        functools.partial(_attn_sample_kernel, dils=dils, dec=dec),
        grid=(seqs // per,),
        in_specs=in_specs,
        out_specs=[out_row] * (2 * len(dils)),
        out_shape=[jax.ShapeDtypeStruct((seqs * SAMPLE_ROWS, GROUP_COLS), F32)] * (2 * len(dils)),
        compiler_params=_params(1),
        name=name,
    )(*new_rows, *windows)


def kernel(x_prompt, x_sample, cache_kv1, cache_kv2, cache_kv3, state_conv, ffn1_norm, ffn1_w_in,
           ffn1_w_out, mix_norm, w_in, b_gate, conv_w, w_attn_out, w_conv_out, w_out, ffn2_norm,
           ffn2_w_in, ffn2_w_out, final_norm):
    batch, seq, _ = x_prompt.shape
    n_seq, dec, _ = x_sample.shape
    depth = w_in.shape[0]
    caches = (cache_kv1, cache_kv2, cache_kv3)
    assert dec <= SAMPLE_ROWS - 2 and seq % (GROUPS[-1][1] * QBLOCK) == 0
    for cache, (win, dil) in zip(caches, GROUPS):
        assert cache.shape[2] == win and win == SPAN * dil and (dil == 1 or dec <= dil)

    f1_wi, f1_wo = ffn1_w_in.astype(BF16), ffn1_w_out.astype(BF16)
    f2_wi, f2_wo = ffn2_w_in.astype(BF16), ffn2_w_out.astype(BF16)
    w_qkv, w_conv, w_gate = _cast_split_call(w_in, (QKV_DIM, 3 * CONV_DIM, 2 * D_MODEL), LANES,
                                             "cast_w_in")
    w_ao = w_attn_out.astype(BF16)
    w_co = w_conv_out.astype(BF16)
    w_o = w_out.astype(BF16)
    g1 = ffn1_norm.reshape(depth, 1, D_MODEL)
    gm = mix_norm.reshape(depth, 1, D_MODEL)
    g2 = ffn2_norm.reshape(depth, 1, D_MODEL)
    gf = final_norm.reshape(1, D_MODEL)
    bg = b_gate.reshape(depth, 1, 2 * D_MODEL)

    caches_t = [c.transpose(0, 1, 3, 4, 5, 2).reshape(depth, n_seq, 2, GROUP_COLS, c.shape[2])
                for c in caches]

    xp = x_prompt.reshape(batch * seq, D_MODEL)
    xs = jnp.pad(x_sample, ((0, 0), (0, SAMPLE_ROWS - dec), (0, 0))).reshape(n_seq * SAMPLE_ROWS, D_MODEL)
    ns = xs.shape[0]
    tail_rows = GROUPS[-1][0]
    dils = tuple(dil for _, dil in GROUPS)

    kv_prompt = [[] for _ in GROUPS]
    conv_prompt, conv_sample = [], []
    new_caches = None
    for l in range(depth):
        last = l == depth - 1
        xs = _ffn_call(xs, g1, f1_wi, f1_wo, gf, l, ns, False, f"s_ffn1_{l}")[0]
        s_qkv = _qkv_call(xs, gm, w_qkv, l, ns, ns, ns, (1,) * N_GROUPS, f"s_qkv_{l}")[N_GROUPS:]
        s_tail = [a.reshape(ns, 3 * GROUP_COLS) for a in s_qkv]

        outs = _ffn_call(xp, g1, f1_wi, f1_wo, gf, l, ROW_TILE, False, f"p_ffn1_{l}",
                         roll=(s_tail, caches_t, new_caches, 0, dec))
        xp, new_caches, windows_a = outs[0], outs[1:1 + N_GROUPS], outs[1 + N_GROUPS:]
        p_qkv = _qkv_call(xp, gm, w_qkv, l, ROW_TILE, seq, tail_rows, dils, f"p_qkv_{l}")
        p_attn = []
        for g, (win, dil) in enumerate(GROUPS):
            p_attn.append(_attn_prompt_call(p_qkv[g], f"p_attn{g}_{l}"))
            keep = min(win, seq)
            tail = p_qkv[N_GROUPS + g].transpose(0, 2, 1, 3).reshape(batch, tail_rows, 3 * GROUP_COLS)
            k_t = tail[:, tail_rows - keep:, GROUP_COLS:2 * GROUP_COLS]
            v_t = tail[:, tail_rows - keep:, 2 * GROUP_COLS:]
            kv_prompt[g].append(jnp.stack([k_t, v_t], axis=2).reshape(batch, keep, 2, HEADS, HEAD_DIM))
        xp, p_p = _mixer_call(xp, gm, w_conv, w_gate, bg, conv_w, w_co, w_ao, w_o, None, p_attn, l,
                              ROW_TILE, seq, f"p_mix_{l}")
        conv_prompt.append(p_p.reshape(batch, 8, CONV_DIM)[:, 6:8])
        outs = _ffn_call(xp, g2, f2_wi, f2_wo, gf, l, ROW_TILE, last, f"p_ffn2_{l}",
                         roll=(s_tail, caches_t, new_caches, n_seq // 2, dec))
        xp, new_caches, windows_b = outs[0], outs[1:1 + N_GROUPS], outs[1 + N_GROUPS:]

        attn_a = _attn_sample_call(s_tail, windows_a, 0, dils, dec, f"s_attn_a_{l}")
        attn_b = _attn_sample_call(s_tail, windows_b, n_seq // 2, dils, dec, f"s_attn_b_{l}")
        s_attn = [tuple(jnp.concatenate([attn_a[2 * g + j], attn_b[2 * g + j]], axis=0)
                        .reshape(1, 1, ns, GROUP_COLS) for j in range(2)) for g in range(N_GROUPS)]
        st = jnp.pad(state_conv[l], ((0, 0), (SAMPLE_ROWS - 2, 0), (0, 0)))
        st = jnp.roll(st, -1, axis=0).reshape(ns, CONV_DIM)
        xs, s_p = _mixer_call(xs, gm, w_conv, w_gate, bg, conv_w, w_co, w_ao, w_o, st, s_attn, l,
                              ns, ns, f"s_mix_{l}")
        conv_sample.append(s_p.reshape(n_seq, SAMPLE_ROWS, CONV_DIM)[:, dec - 2:dec])
        xs = _ffn_call(xs, g2, f2_wi, f2_wo, gf, l, ns, last, f"s_ffn2_{l}")[0]

    y_prompt = xp.reshape(batch, seq, D_MODEL)
    y_sample = xs.reshape(n_seq, SAMPLE_ROWS, D_MODEL)[:, :dec]
    kv_sample = [c.reshape(depth, n_seq, 2, HEADS, HEAD_DIM, c.shape[-1]).transpose(0, 1, 5, 2, 3, 4)
                 for c in new_caches]
    return (y_prompt, y_sample,
            jnp.stack(kv_prompt[0]), jnp.stack(kv_prompt[1]), jnp.stack(kv_prompt[2]),
            jnp.stack(conv_prompt),
            kv_sample[0], kv_sample[1], kv_sample[2],
            jnp.stack(conv_sample))
```

```python
import functools

import jax
import jax.numpy as jnp
from jax import lax
from jax.experimental import pallas as pl
from jax.experimental.pallas import tpu as pltpu

F32 = jnp.float32
BF16 = jnp.bfloat16

D_MODEL = 1024
D_FF = 2816
HEAD_DIM = 64
HEADS = 8
GROUP_COLS = HEADS * HEAD_DIM
HALF_COLS = GROUP_COLS // 2
GROUPS = ((128, 1), (512, 4), (2048, 16))
N_GROUPS = len(GROUPS)
SPAN = 128
ATTN_DIM = N_GROUPS * GROUP_COLS
QKV_DIM = 3 * ATTN_DIM
CONV_DIM = D_MODEL
RMS_EPS = 1e-6
ATTN_SCALE = HEAD_DIM ** -0.5
LOG2_E = 1.4426950408889634
LN_2 = 0.6931471805599453
NEG = -1e30

FF_CHUNK = 256
N_FF_CHUNKS = D_FF // FF_CHUNK
QBLOCK = 128
MAX_ATTN_BLOCKS = 16
LANES = 128
LANE_TILES = GROUP_COLS // LANES
SAMPLE_ROWS = 8
SAMPLE_SEQS_PER_STEP = 2
V7X_VMEM_LIMIT_BYTES = 56 * 1024 * 1024
ROW_TILE = 512


def _params(n_axes):
    return pltpu.CompilerParams(
        dimension_semantics=("arbitrary",) * n_axes,
        vmem_limit_bytes=V7X_VMEM_LIMIT_BYTES)


def _resident(block_shape, index_map):
    return pl.BlockSpec(block_shape, index_map, pipeline_mode=pl.Buffered(1))


def _rms_inv(x):
    return lax.rsqrt(jnp.mean(x * x, axis=-1, keepdims=True) + RMS_EPS)


def _cast_split_kernel(w_ref, *out_refs):
    off = 0
    for out in out_refs:
        width = out.shape[-1]
        out[...] = w_ref[:, off:off + width].astype(BF16)
        off += width


def _cast_split_call(w, widths, rows, name):
    depth, r, c = w.shape
    assert sum(widths) == c and r % rows == 0
    return pl.pallas_call(
        _cast_split_kernel,
        grid=(depth, r // rows),
        in_specs=[pl.BlockSpec((None, rows, c), lambda l, i: (l, i, 0))],
        out_specs=[pl.BlockSpec((None, rows, wd), lambda l, i: (l, i, 0)) for wd in widths],
        out_shape=[jax.ShapeDtypeStruct((depth, r, wd), BF16) for wd in widths],
        compiler_params=_params(2),
        name=name,
    )(w)


def _roll_window(c_ref, new_rows, out_ref, dec):
    lb = c_ref.shape[1]
    rolled = pltpu.roll(c_ref[...], lb - dec, axis=1)
    out_ref[...] = rolled
    sub = lax.broadcasted_iota(jnp.int32, new_rows.shape, 0)
    shifted = jnp.where(sub >= SAMPLE_ROWS - dec,
                        pltpu.roll(new_rows, SAMPLE_ROWS - dec, axis=0), 0.0)
    padded = jnp.concatenate(
        [jnp.zeros((LANES - SAMPLE_ROWS, new_rows.shape[1]), F32), shifted], axis=0)
    new_t = padded.T
    lane_t = lax.broadcasted_iota(jnp.int32, new_t.shape, 1)
    out_ref[:, lb - LANES:] = jnp.where(lane_t >= LANES - dec, new_t, rolled[:, lb - LANES:])


def _ffn_kernel(*refs, final, dec, n_alias):
    x_ref, g_ref, wi_ref, wo_ref, fg_ref = refs[:5]
    refs = refs[5:]
    if dec:
        new_refs, cache_refs = refs[:N_GROUPS], refs[N_GROUPS:2 * N_GROUPS]
        refs = refs[2 * N_GROUPS + n_alias:]
        o_ref, roll_refs, refs = refs[0], refs[1:1 + N_GROUPS], refs[1 + N_GROUPS:]
        half_refs, refs = refs[:N_GROUPS], refs[N_GROUPS:]
    else:
        o_ref, refs = refs[0], refs[1:]
    h_s, acc_s = refs
    if dec:
        is_hi = (lax.broadcasted_iota(jnp.int32, (SAMPLE_ROWS, HALF_COLS), 0) * 0
                 + pl.program_id(0) % 2) == 1

        for g in range(N_GROUPS):
            kn, vn = (jnp.where(is_hi, new_refs[g][:, k * GROUP_COLS + HALF_COLS:(k + 1) * GROUP_COLS],
                                new_refs[g][:, k * GROUP_COLS:k * GROUP_COLS + HALF_COLS])
                      for k in (1, 2))
            _roll_window(cache_refs[g].at[0], kn, roll_refs[g].at[0], dec)
            _roll_window(cache_refs[g].at[1], vn, roll_refs[g].at[1], dec)
            half_refs[g][...] = cache_refs[g][...].astype(BF16)
    x = x_ref[...]
    h_s[...] = ((x * _rms_inv(x)) * g_ref[...]).astype(BF16)
    for c in range(N_FF_CHUNKS):
        lo = c * FF_CHUNK
        gate = jnp.dot(h_s[...], wi_ref[:, lo:lo + FF_CHUNK], preferred_element_type=F32)
        up = jnp.dot(h_s[...], wi_ref[:, D_FF + lo:D_FF + lo + FF_CHUNK],
                     preferred_element_type=F32)
        act = ((gate * jax.nn.sigmoid(gate)) * up).astype(BF16)
        part = jnp.dot(act, wo_ref[lo:lo + FF_CHUNK, :], preferred_element_type=F32)
        if c == 0:
            acc_s[...] = part
        else:
            acc_s[...] += part
    y = x_ref[...] + 0.5 * acc_s[...]
    if final:
        y = (y * _rms_inv(y)) * fg_ref[...]
    o_ref[...] = y


def _ffn_call(x, gains, wi, wo, final_gain, layer, tm, final, name, roll=None):
    n = x.shape[0]
    in_specs = [
        pl.BlockSpec((tm, D_MODEL), lambda i: (i, 0)),
        pl.BlockSpec((None, 1, D_MODEL), lambda i: (layer, 0, 0)),
        _resident((None, D_MODEL, 2 * D_FF), lambda i: (layer, 0, 0)),
        _resident((None, D_FF, D_MODEL), lambda i: (layer, 0, 0)),
        pl.BlockSpec((1, D_MODEL), lambda i: (0, 0)),
    ]
    args = [x, gains, wi, wo, final_gain]
    out_specs = [pl.BlockSpec((tm, D_MODEL), lambda i: (i, 0))]
    out_shape = [jax.ShapeDtypeStruct((n, D_MODEL), F32)]
    aliases, dec, n_alias = {}, 0, 0
    scratch = [pltpu.VMEM((tm, D_MODEL), BF16), pltpu.VMEM((tm, D_MODEL), F32)]
    if roll is not None:
        new_rows, caches, prev_outs, seq_base, dec = roll
        assert 2 * (caches[0].shape[1] // 2) == n // tm
        in_specs += [pl.BlockSpec((SAMPLE_ROWS, 3 * GROUP_COLS), lambda i: (seq_base + i // 2, 0))
                     for _ in new_rows]
        args += list(new_rows)
        slab_specs = [pl.BlockSpec((None, None, 2, HALF_COLS, c.shape[-1]),
                                   lambda i: (layer, seq_base + i // 2, 0, i % 2, 0)) for c in caches]
        in_specs += slab_specs
        args += list(caches)
        if prev_outs is not None:
            n_alias = N_GROUPS
            for g, prev in enumerate(prev_outs):
                aliases[len(args)] = 1 + g
                in_specs.append(pl.BlockSpec(memory_space=pl.ANY))
                args.append(prev)
        out_specs += slab_specs
        out_shape += [jax.ShapeDtypeStruct(c.shape, F32) for c in caches]
        out_specs += [pl.BlockSpec((None, 2, HALF_COLS, c.shape[-1]), lambda i: (i // 2, 0, i % 2, 0))
                      for c in caches]
        out_shape += [jax.ShapeDtypeStruct((n // tm // 2, 2, GROUP_COLS, c.shape[-1]), BF16)
                      for c in caches]
    return pl.pallas_call(
        functools.partial(_ffn_kernel, final=final, dec=dec, n_alias=n_alias),
        grid=(n // tm,),
        in_specs=in_specs,
        out_specs=out_specs,
        out_shape=out_shape,
        input_output_aliases=aliases,
        scratch_shapes=scratch,
        compiler_params=_params(1),
        name=name,
    )(*args)


def _qkv_kernel(x_ref, g_ref, w_ref, *refs, dils):
    n_g = len(dils)
    bf_refs, f32_refs = refs[:n_g], refs[n_g:2 * n_g]
    u_s, slab_s = refs[2 * n_g], refs[2 * n_g + 1]
    perm_refs = list(refs[2 * n_g + 2:])
    tm = x_ref.shape[0]
    x = x_ref[...]
    u = (x * _rms_inv(x)) * g_ref[...]
    u_s[...] = u.astype(BF16)
    if perm_refs:
        for t in range(D_MODEL // LANES):
            slab_s[t] = u[:, t * LANES:(t + 1) * LANES]
    for g, dil in enumerate(dils):
        n = tm // dil
        lhs_ref = u_s
        if dil > 1:
            lhs_ref = perm_refs.pop()
            for r in range(dil):
                for t in range(D_MODEL // LANES):
                    lhs_ref[r * n:(r + 1) * n, t * LANES:(t + 1) * LANES] = (
                        slab_s[t, pl.ds(r, n, stride=dil), :].astype(BF16))
        even = (lax.broadcasted_iota(jnp.int32, (n, GROUP_COLS), 1) & (2 * HEAD_DIM - 1)) < HEAD_DIM
        for kind in range(3):
            col = (kind * n_g + g) * GROUP_COLS
            z = jnp.dot(lhs_ref[...], w_ref[:, col:col + GROUP_COLS], preferred_element_type=F32)
            if kind == 0:
                z = z * ATTN_SCALE
            for r in range(dil):
                zr = z[r * n:(r + 1) * n]
                first_kind = 3 - f32_refs[g].shape[0] // LANE_TILES
                for t in range(LANE_TILES if kind >= first_kind else 0):
                    f32_refs[g][(kind - first_kind) * LANE_TILES + t, pl.ds(r, n, stride=dil), :] = (
                        zr[:, t * LANES:(t + 1) * LANES])
                if kind == 0:
                    zq = zr * LOG2_E
                    bf_refs[g][r, :, :GROUP_COLS] = jnp.where(even, zq, 0.0).astype(BF16)
                    bf_refs[g][r, :, GROUP_COLS:2 * GROUP_COLS] = jnp.where(even, 0.0, zq).astype(BF16)
                else:
                    bf_refs[g][r, :, (kind + 1) * GROUP_COLS:(kind + 2) * GROUP_COLS] = zr.astype(BF16)


def _qkv_call(x, gains, w, layer, tm, seq_rows, tail_rows, dils, f32_parts, name):
    n = x.shape[0]
    tiles_per_seq = seq_rows // tm
    tail_tiles = tail_rows // tm
    n_seq = n // seq_rows

    def tail_map(i):
        return (i // tiles_per_seq, 0,
                jnp.maximum(i % tiles_per_seq - (tiles_per_seq - tail_tiles), 0), 0)

    specs = [pl.BlockSpec((None, dil, tm // dil, 4 * GROUP_COLS),
                          lambda i: (i // tiles_per_seq, 0, i % tiles_per_seq, 0)) for dil in dils]
    specs += [pl.BlockSpec((None, f32_parts * LANE_TILES, tm, LANES), tail_map) for _ in dils]
    shapes = [jax.ShapeDtypeStruct((n_seq, dil, seq_rows // dil, 4 * GROUP_COLS), BF16) for dil in dils]
    shapes += [jax.ShapeDtypeStruct((n_seq, f32_parts * LANE_TILES, tail_rows, LANES), F32)
               for _ in dils]
    n_perm = sum(dil > 1 for dil in dils)
    return pl.pallas_call(
        functools.partial(_qkv_kernel, dils=dils),
        grid=(n // tm,),
        in_specs=[
            pl.BlockSpec((tm, D_MODEL), lambda i: (i, 0)),
            pl.BlockSpec((None, 1, D_MODEL), lambda i: (layer, 0, 0)),
            _resident((None, D_MODEL, QKV_DIM), lambda i: (layer, 0, 0)),
        ],
        out_specs=specs,
        out_shape=shapes,
        scratch_shapes=[pltpu.VMEM((tm, D_MODEL), BF16),
                        pltpu.VMEM((D_MODEL // LANES, tm, LANES), F32)]
                       + [pltpu.VMEM((tm, D_MODEL), BF16)] * n_perm,
        compiler_params=_params(1),
        name=name,
    )(x, gains, w)


CONV_CHUNK = 256
GATE_CHUNK = 512


def _mixer_kernel(*refs, sample, tiles_per_seq):
    x_ref, g_ref, wc_ref, wg_ref, bg_ref, cw_ref, wco_ref, wao_ref, wo_ref = refs[:9]
    refs = refs[9:]
    if sample:
        st_ref, refs = refs[0], refs[1:]
    attn_refs, refs = refs[:2 * N_GROUPS], refs[2 * N_GROUPS:]
    out_ref, p_ref, u_s, c_s, m_s = refs[:5]
    scratch = list(refs[5:])
    if not sample:
        halo_s = scratch.pop(0)
    tm = x_ref.shape[0]
    x = x_ref[...]
    u_s[...] = ((x * _rms_inv(x)) * g_ref[...]).astype(BF16)
    row = lax.broadcasted_iota(jnp.int32, (tm, CONV_CHUNK), 0)

    if not sample:
        @pl.when(pl.program_id(0) % tiles_per_seq == 0)
        def _():
            halo_s[...] = jnp.zeros_like(halo_s)

    for c in range(CONV_DIM // CONV_CHUNK):
        sl = slice(c * CONV_CHUNK, (c + 1) * CONV_CHUNK)
        u = u_s[...]
        cb = jnp.dot(u, wc_ref[:, sl], preferred_element_type=F32)
        cc = jnp.dot(u, wc_ref[:, CONV_DIM + c * CONV_CHUNK:CONV_DIM + (c + 1) * CONV_CHUNK],
                     preferred_element_type=F32)
        ch = jnp.dot(u, wc_ref[:, 2 * CONV_DIM + c * CONV_CHUNK:2 * CONV_DIM + (c + 1) * CONV_CHUNK],
                     preferred_element_type=F32)
        p = cc * ch
        if sample:
            p_ref[:, sl] = p
            p = jnp.where((row & (SAMPLE_ROWS - 1)) >= SAMPLE_ROWS - 2, st_ref[:, sl], p)
            r1 = pltpu.roll(p, 1, axis=0)
            r2 = pltpu.roll(p, 2, axis=0)
        else:
            r1 = pltpu.roll(p, 1, axis=0)
            r2 = pltpu.roll(p, 2, axis=0)
            h = halo_s[:, sl]
            r1 = jnp.where(row == 0, h[7:8], r1)
            r2 = jnp.where(row == 0, h[6:7], jnp.where(row == 1, h[7:8], r2))
            halo_s[:, sl] = p[tm - 8:tm]
            p_ref[:, sl] = p[tm - 8:tm]
        conv = cw_ref[0:1, sl] * r2 + cw_ref[1:2, sl] * r1 + cw_ref[2:3, sl] * p
        c_s[:, sl] = (cb * conv).astype(BF16)

    def natural(ref):
        dil = ref.shape[0]
        if dil == 1:
            return ref[0]
        s_ref = scratch.pop()
        for r in range(dil):
            for t in range(LANE_TILES):
                s_ref[t, pl.ds(r, tm // dil, stride=dil), :] = ref[r, :, t * LANES:(t + 1) * LANES]
        return jnp.concatenate([s_ref[t] for t in range(LANE_TILES)], axis=1)

    o = [natural(attn_refs[2 * g]) for g in range(N_GROUPS)]
    lse = [natural(attn_refs[2 * g + 1]) for g in range(N_GROUPS)]
    mx = jnp.maximum(jnp.maximum(lse[0], lse[1]), lse[2])
    e = [jnp.exp2(l - mx) for l in lse]
    a = (e[0] * o[0] + e[1] * o[1] + e[2] * o[2]) / (e[0] + e[1] + e[2])
    aproj = jnp.dot(a.astype(BF16), wao_ref[...], preferred_element_type=F32)
    cproj = jnp.dot(c_s[...], wco_ref[...], preferred_element_type=F32)
    for c in range(D_MODEL // GATE_CHUNK):
        sl = slice(c * GATE_CHUNK, (c + 1) * GATE_CHUNK)
        za = jnp.dot(u_s[...], wg_ref[:, sl], preferred_element_type=F32) + bg_ref[:, sl]
        zc = (jnp.dot(u_s[...], wg_ref[:, D_MODEL + c * GATE_CHUNK:D_MODEL + (c + 1) * GATE_CHUNK],
                      preferred_element_type=F32)
              + bg_ref[:, D_MODEL + c * GATE_CHUNK:D_MODEL + (c + 1) * GATE_CHUNK])
        m = jax.nn.sigmoid(za) * aproj[:, sl] + jax.nn.sigmoid(zc) * cproj[:, sl]
        m_s[:, sl] = m.astype(BF16)
    out_ref[...] = x_ref[...] + jnp.dot(m_s[...], wo_ref[...], preferred_element_type=F32)


def _mixer_call(x, gains, wc, wg, bg, cw, wco, wao, wo, state, attn, layer, tm, seq_rows, name):
    n = x.shape[0]
    sample = state is not None
    tiles_per_seq = seq_rows // tm
    row_spec = pl.BlockSpec((tm, D_MODEL), lambda i: (i, 0))
    in_specs = [
        row_spec,
        pl.BlockSpec((None, 1, D_MODEL), lambda i: (layer, 0, 0)),
        _resident((None, D_MODEL, 3 * CONV_DIM), lambda i: (layer, 0, 0)),
        _resident((None, D_MODEL, 2 * D_MODEL), lambda i: (layer, 0, 0)),
        pl.BlockSpec((None, 1, 2 * D_MODEL), lambda i: (layer, 0, 0)),
        pl.BlockSpec((None, 3, CONV_DIM), lambda i: (layer, 0, 0)),
        _resident((None, CONV_DIM, D_MODEL), lambda i: (layer, 0, 0)),
        _resident((None, GROUP_COLS, D_MODEL), lambda i: (layer, 0, 0)),
        _resident((None, D_MODEL, D_MODEL), lambda i: (layer, 0, 0)),
    ]
    args = [x, gains, wc, wg, bg, cw, wco, wao, wo]
    scratch = [pltpu.VMEM((tm, D_MODEL), BF16), pltpu.VMEM((tm, CONV_DIM), BF16),
               pltpu.VMEM((tm, D_MODEL), BF16)]
    if sample:
        in_specs.append(row_spec)
        args.append(state)
        p_spec = row_spec
        p_shape = jax.ShapeDtypeStruct((n, CONV_DIM), F32)
    else:
        p_spec = pl.BlockSpec((8, CONV_DIM), lambda i: (i // tiles_per_seq, 0))
        p_shape = jax.ShapeDtypeStruct((n // seq_rows * 8, CONV_DIM), F32)
        scratch.append(pltpu.VMEM((8, CONV_DIM), F32))
    flat = [a for pair in attn for a in pair]
    in_specs += [pl.BlockSpec((None, a.shape[1], tm // a.shape[1], GROUP_COLS),
                              lambda i: (i // tiles_per_seq, 0, i % tiles_per_seq, 0))
                 for a in flat]
    scratch += [pltpu.VMEM((LANE_TILES, tm, LANES), F32)] * sum(a.shape[1] > 1 for a in flat)
    return pl.pallas_call(
        functools.partial(_mixer_kernel, sample=sample, tiles_per_seq=tiles_per_seq),
        grid=(n // tm,),
        in_specs=in_specs,
        out_specs=[row_spec, p_spec],
        out_shape=[jax.ShapeDtypeStruct((n, D_MODEL), F32), p_shape],
        scratch_shapes=scratch,
        compiler_params=_params(1),
        name=name,
    )(*args, *flat)


def _attn_prompt_kernel(qe_ref, qo_ref, kp_ref, kc_ref, vp_ref, vc_ref, o_ref, l_ref):
    n = pl.program_id(2)
    qi = lax.broadcasted_iota(jnp.int32, (QBLOCK, 2 * QBLOCK), 0)
    col = lax.broadcasted_iota(jnp.int32, (QBLOCK, 2 * QBLOCK), 1)
    cur_ok = (col >= QBLOCK) & (col - QBLOCK <= qi)
    bias_inner = jnp.where(cur_ok | ((col < QBLOCK) & (col >= qi)), 0.0, NEG)
    bias_first = jnp.where(cur_ok | ((col < QBLOCK) & (col >= qi + jnp.where(n > 0, 0, QBLOCK))),
                           0.0, NEG)
    even = lax.broadcasted_iota(jnp.int32, (QBLOCK, 2 * HEAD_DIM), 1) < HEAD_DIM
    streams, step_rows, _ = qe_ref.shape
    for st, i in [(st, i) for st in range(streams) for i in range(step_rows // QBLOCK)]:
        rows = slice(i * QBLOCK, (i + 1) * QBLOCK)
        if i == 0:
            k = jnp.concatenate([kp_ref[st], kc_ref[st, rows]], axis=0)
            v = jnp.concatenate([vp_ref[st], vc_ref[st, rows]], axis=0)
            bias = bias_first
        else:
            k = kc_ref[st, (i - 1) * QBLOCK:(i + 1) * QBLOCK]
            v = vc_ref[st, (i - 1) * QBLOCK:(i + 1) * QBLOCK]
            bias = bias_inner
        for j in range(HEADS // 2):
            sl = slice(2 * HEAD_DIM * j, 2 * HEAD_DIM * (j + 1))
            halves = []
            for q_ref in (qe_ref, qo_ref):
                s = lax.dot_general(q_ref[st, rows, sl], k[:, sl], (((1,), (1,)), ((), ())),
                                    preferred_element_type=F32) + bias
                m = jnp.max(s, axis=1, keepdims=True)
                p = jnp.exp2(s - m)
                l = jnp.sum(p, axis=1, keepdims=True)
                pv = jnp.dot(p.astype(BF16), v[:, sl], preferred_element_type=F32)
                halves.append((pv / l, jnp.broadcast_to(m + jnp.log2(l), pv.shape)))
            o_ref[st, rows, sl] = jnp.where(even, halves[0][0], halves[1][0])
            l_ref[st, rows, sl] = jnp.where(even, halves[0][1], halves[1][1])


def _attn_prompt_call(qkv, name):
    batch, dil, length, _ = qkv.shape
    blocks = min(MAX_ATTN_BLOCKS, length // QBLOCK)
    streams = min(dil, MAX_ATTN_BLOCKS // blocks)
    step_rows = blocks * QBLOCK
    nb = length // step_rows
    assert length % step_rows == 0 and dil % streams == 0

    def spec(part, prev=False):
        if prev:
            return pl.BlockSpec((None, streams, QBLOCK, GROUP_COLS),
                                lambda b, r, n: (b, r, jnp.maximum(blocks * n - 1, 0), part))
        return pl.BlockSpec((None, streams, step_rows, GROUP_COLS), lambda b, r, n: (b, r, n, part))

    out_spec = pl.BlockSpec((None, streams, step_rows, GROUP_COLS), lambda b, r, n: (b, r, n, 0))
    out_shape = jax.ShapeDtypeStruct((batch, dil, length, GROUP_COLS), F32)
    return pl.pallas_call(
        _attn_prompt_kernel,
        grid=(batch, dil // streams, nb),
        in_specs=[spec(0), spec(1), spec(2, True), spec(2), spec(3, True), spec(3)],
        out_specs=[out_spec, out_spec],
        out_shape=[out_shape, out_shape],
        compiler_params=_params(3),
        name=name,
    )(qkv, qkv, qkv, qkv, qkv, qkv)


def _sample_attn(q, kn, vn, kt, vt, o_ref, l_ref, dil, dec):
    width, lb = kt.shape
    sub = lax.broadcasted_iota(jnp.int32, (SAMPLE_ROWS, width), 0)
    lane = lax.broadcasted_iota(jnp.int32, (SAMPLE_ROWS, width), 1)
    diag = sub == lane // HEAD_DIM
    qb = jnp.concatenate(
        [jnp.where(diag, jnp.broadcast_to(q[t:t + 1], diag.shape), 0.0) for t in range(dec)],
        axis=0).astype(BF16)
    rows = dec * SAMPLE_ROWS
    s = jnp.dot(qb, kt, preferred_element_type=F32)
    t_of = lax.broadcasted_iota(jnp.int32, (rows, lb), 0) // SAMPLE_ROWS
    diff = lax.broadcasted_iota(jnp.int32, (rows, lb), 1) - t_of
    valid = (diff >= 0) & ((diff & (dil - 1)) == 0)
    s = jnp.where(valid, s, NEG)
    m = jnp.max(s, axis=1, keepdims=True)

    qbf = qb.astype(F32)
    knr = kn.astype(BF16).astype(F32)
    vnr = vn.astype(BF16).astype(F32)
    t_col = lax.broadcasted_iota(jnp.int32, (rows, 1), 0) // SAMPLE_ROWS
    s_new = []
    for t2 in range(dec):
        s2 = jnp.sum(qbf * knr[t2:t2 + 1], axis=1, keepdims=True)
        d2 = t_col - t2
        s2 = jnp.where((d2 >= 0) & ((d2 & (dil - 1)) == 0), s2, NEG)
        s_new.append(s2)
        m = jnp.maximum(m, s2)
    p = jnp.exp(s - m)
    l = jnp.sum(p, axis=1, keepdims=True)
    acc = lax.dot_general(p.astype(BF16), vt, (((1,), (1,)), ((), ())),
                          preferred_element_type=F32)
    for t2 in range(dec):
        pn = jnp.exp(s_new[t2] - m)
        l = l + pn
        acc = acc + pn * vnr[t2:t2 + 1]
    o = acc / l
    lse = jnp.broadcast_to((m + jnp.log(l)) * LOG2_E, o.shape)
    o_rows, l_rows = [], []
    for t in range(dec):
        blk = slice(t * SAMPLE_ROWS, (t + 1) * SAMPLE_ROWS)
        o_rows.append(jnp.sum(jnp.where(diag, o[blk], 0.0), axis=0, keepdims=True))
        l_rows.append(jnp.sum(jnp.where(diag, lse[blk], 0.0), axis=0, keepdims=True))
    pad = jnp.zeros((SAMPLE_ROWS - dec, width), F32)
    o_ref[...] = jnp.concatenate(o_rows + [pad], axis=0)
    l_ref[...] = jnp.concatenate(l_rows + [pad], axis=0)


def _attn_sample_kernel(*refs, dils, dec):
    n_g = len(dils)
    for j in range(refs[n_g].shape[0]):
        rows = slice(j * SAMPLE_ROWS, (j + 1) * SAMPLE_ROWS)
        for g, dil in enumerate(dils):
            new_ref, c_ref = refs[g], refs[n_g + g]
            o_ref, l_ref = refs[2 * n_g + 2 * g:2 * n_g + 2 * g + 2]
            q, kn, vn = (new_ref[rows, k * GROUP_COLS:(k + 1) * GROUP_COLS] for k in range(3))
            _sample_attn(q, kn, vn, c_ref[j, 0], c_ref[j, 1], o_ref.at[rows], l_ref.at[rows], dil, dec)


def _attn_sample_call(new_rows, windows, seq_base, dils, dec, name):
    seqs = windows[0].shape[0]
    per = SAMPLE_SEQS_PER_STEP
    assert seqs % per == 0 and seq_base % per == 0
    in_specs = [pl.BlockSpec((per * SAMPLE_ROWS, 3 * GROUP_COLS), lambda s: (seq_base // per + s, 0))
                for _ in new_rows]
    in_specs += [pl.BlockSpec((per, 2, GROUP_COLS, w.shape[-1]), lambda s: (s, 0, 0, 0))
                 for w in windows]
    out_row = pl.BlockSpec((per * SAMPLE_ROWS, GROUP_COLS), lambda s: (s, 0))
    return pl.pallas_call(
        functools.partial(_attn_sample_kernel, dils=dils, dec=dec),
        grid=(seqs // per,),
        in_specs=in_specs,
        out_specs=[out_row] * (2 * len(dils)),
        out_shape=[jax.ShapeDtypeStruct((seqs * SAMPLE_ROWS, GROUP_COLS), F32)] * (2 * len(dils)),
        compiler_params=_params(1),
        name=name,
    )(*new_rows, *windows)


def kernel(x_prompt, x_sample, cache_kv1, cache_kv2, cache_kv3, state_conv, ffn1_norm, ffn1_w_in,
           ffn1_w_out, mix_norm, w_in, b_gate, conv_w, w_attn_out, w_conv_out, w_out, ffn2_norm,
           ffn2_w_in, ffn2_w_out, final_norm):
    batch, seq, _ = x_prompt.shape
    n_seq, dec, _ = x_sample.shape
    depth = w_in.shape[0]
    caches = (cache_kv1, cache_kv2, cache_kv3)
    assert dec <= SAMPLE_ROWS - 2 and seq % (GROUPS[-1][1] * QBLOCK) == 0
    for cache, (win, dil) in zip(caches, GROUPS):
        assert cache.shape[2] == win and win == SPAN * dil and (dil == 1 or dec <= dil)

    f1_wi, f1_wo = ffn1_w_in.astype(BF16), ffn1_w_out.astype(BF16)
    f2_wi, f2_wo = ffn2_w_in.astype(BF16), ffn2_w_out.astype(BF16)
    w_qkv, w_conv, w_gate = _cast_split_call(w_in, (QKV_DIM, 3 * CONV_DIM, 2 * D_MODEL), LANES,
                                             "cast_w_in")
    w_ao = w_attn_out.astype(BF16)
    w_co = w_conv_out.astype(BF16)
    w_o = w_out.astype(BF16)
    g1 = ffn1_norm.reshape(depth, 1, D_MODEL)
    gm = mix_norm.reshape(depth, 1, D_MODEL)
    g2 = ffn2_norm.reshape(depth, 1, D_MODEL)
    gf = final_norm.reshape(1, D_MODEL)
    bg = b_gate.reshape(depth, 1, 2 * D_MODEL)

    caches_t = [c.transpose(0, 1, 3, 4, 5, 2).reshape(depth, n_seq, 2, GROUP_COLS, c.shape[2])
                for c in caches]

    xp = x_prompt.reshape(batch * seq, D_MODEL)
    xs = jnp.pad(x_sample, ((0, 0), (0, SAMPLE_ROWS - dec), (0, 0))).reshape(n_seq * SAMPLE_ROWS, D_MODEL)
    ns = xs.shape[0]
    tail_rows = GROUPS[-1][0]
    dils = tuple(dil for _, dil in GROUPS)

    kv_prompt = [[] for _ in GROUPS]
    conv_prompt, conv_sample = [], []
    new_caches = None
    for l in range(depth):
        last = l == depth - 1
        xs = _ffn_call(xs, g1, f1_wi, f1_wo, gf, l, ns, False, f"s_ffn1_{l}")[0]
        s_qkv = _qkv_call(xs, gm, w_qkv, l, ns, ns, ns, (1,) * N_GROUPS, 3, f"s_qkv_{l}")[N_GROUPS:]
        s_tail = [a[0].transpose(1, 0, 2).reshape(ns, 3 * GROUP_COLS) for a in s_qkv]

        outs = _ffn_call(xp, g1, f1_wi, f1_wo, gf, l, ROW_TILE, False, f"p_ffn1_{l}",
                         roll=(s_tail, caches_t, new_caches, 0, dec))
        xp, new_caches, windows_a = outs[0], outs[1:1 + N_GROUPS], outs[1 + N_GROUPS:]
        p_qkv = _qkv_call(xp, gm, w_qkv, l, ROW_TILE, seq, tail_rows, dils, 2, f"p_qkv_{l}")
        p_attn = []
        for g, (win, dil) in enumerate(GROUPS):
            p_attn.append(_attn_prompt_call(p_qkv[g], f"p_attn{g}_{l}"))
            keep = min(win, seq)
            kv = p_qkv[N_GROUPS + g][:, :, tail_rows - keep:]
            kv = kv.reshape(batch, 2, LANE_TILES, keep, 2, HEAD_DIM).transpose(0, 3, 1, 2, 4, 5)
            kv_prompt[g].append(kv.reshape(batch, keep, 2, HEADS, HEAD_DIM))
        xp, p_p = _mixer_call(xp, gm, w_conv, w_gate, bg, conv_w, w_co, w_ao, w_o, None, p_attn, l,
                              ROW_TILE, seq, f"p_mix_{l}")
        conv_prompt.append(p_p.reshape(batch, 8, CONV_DIM)[:, 6:8])
        outs = _ffn_call(xp, g2, f2_wi, f2_wo, gf, l, ROW_TILE, last, f"p_ffn2_{l}",
                         roll=(s_tail, caches_t, new_caches, n_seq // 2, dec))
        xp, new_caches, windows_b = outs[0], outs[1:1 + N_GROUPS], outs[1 + N_GROUPS:]

        attn_a = _attn_sample_call(s_tail, windows_a, 0, dils, dec, f"s_attn_a_{l}")
        attn_b = _attn_sample_call(s_tail, windows_b, n_seq // 2, dils, dec, f"s_attn_b_{l}")
        s_attn = [tuple(jnp.concatenate([attn_a[2 * g + j], attn_b[2 * g + j]], axis=0)
                        .reshape(1, 1, ns, GROUP_COLS) for j in range(2)) for g in range(N_GROUPS)]
        st = jnp.pad(state_conv[l], ((0, 0), (SAMPLE_ROWS - 2, 0), (0, 0)))
        st = jnp.roll(st, -1, axis=0).reshape(ns, CONV_DIM)
        xs, s_p = _mixer_call(xs, gm, w_conv, w_gate, bg, conv_w, w_co, w_ao, w_o, st, s_attn, l,
                              ns, ns, f"s_mix_{l}")
        conv_sample.append(s_p.reshape(n_seq, SAMPLE_ROWS, CONV_DIM)[:, dec - 2:dec])
        xs = _ffn_call(xs, g2, f2_wi, f2_wo, gf, l, ns, last, f"s_ffn2_{l}")[0]

    y_prompt = xp.reshape(batch, seq, D_MODEL)
    y_sample = xs.reshape(n_seq, SAMPLE_ROWS, D_MODEL)[:, :dec]
    kv_sample = [c.reshape(depth, n_seq, 2, HEADS, HEAD_DIM, c.shape[-1]).transpose(0, 1, 5, 2, 3, 4)
                 for c in new_caches]
    return (y_prompt, y_sample,
            jnp.stack(kv_prompt[0]), jnp.stack(kv_prompt[1]), jnp.stack(kv_prompt[2]),
            jnp.stack(conv_prompt),
            kv_sample[0], kv_sample[1], kv_sample[2],
            jnp.stack(conv_sample))
```

```python
import functools

import jax
import jax.numpy as jnp
from jax import lax
from jax.experimental import pallas as pl
from jax.experimental.pallas import tpu as pltpu

F32 = jnp.float32
BF16 = jnp.bfloat16

D_MODEL = 1024
D_FF = 2816
HEAD_DIM = 64
HEADS = 8
GROUP_COLS = HEADS * HEAD_DIM
HALF_COLS = GROUP_COLS // 2
GROUPS = ((128, 1), (512, 4), (2048, 16))
N_GROUPS = len(GROUPS)
SPAN = 128
ATTN_DIM = N_GROUPS * GROUP_COLS
QKV_DIM = 3 * ATTN_DIM
CONV_DIM = D_MODEL
RMS_EPS = 1e-6
ATTN_SCALE = HEAD_DIM ** -0.5
LOG2_E = 1.4426950408889634
LN_2 = 0.6931471805599453
NEG = -1e30

FF_CHUNK = 256
N_FF_CHUNKS = D_FF // FF_CHUNK
QBLOCK = 128
MAX_ATTN_BLOCKS = 16
LANES = 128
FINE_STRIDE = 4
LANE_TILES = GROUP_COLS // LANES
SAMPLE_ROWS = 8
SAMPLE_SEQS_PER_STEP = 2
V7X_VMEM_LIMIT_BYTES = 56 * 1024 * 1024
ROW_TILE = 512


def _params(n_axes):
    return pltpu.CompilerParams(
        dimension_semantics=("arbitrary",) * n_axes,
        vmem_limit_bytes=V7X_VMEM_LIMIT_BYTES)


def _resident(block_shape, index_map):
    return pl.BlockSpec(block_shape, index_map, pipeline_mode=pl.Buffered(1))


def _rms_inv(x):
    return lax.rsqrt(jnp.mean(x * x, axis=-1, keepdims=True) + RMS_EPS)


def _cast_split_kernel(w_ref, *out_refs):
    off = 0
    for out in out_refs:
        width = out.shape[-1]
        out[...] = w_ref[:, off:off + width].astype(BF16)
        off += width


def _cast_split_call(w, widths, rows, name):
    depth, r, c = w.shape
    assert sum(widths) == c and r % rows == 0
    return pl.pallas_call(
        _cast_split_kernel,
        grid=(depth, r // rows),
        in_specs=[pl.BlockSpec((None, rows, c), lambda l, i: (l, i, 0))],
        out_specs=[pl.BlockSpec((None, rows, wd), lambda l, i: (l, i, 0)) for wd in widths],
        out_shape=[jax.ShapeDtypeStruct((depth, r, wd), BF16) for wd in widths],
        compiler_params=_params(2),
        name=name,
    )(w)


def _roll_window(c_ref, new_rows, out_ref, dec):
    lb = c_ref.shape[1]
    rolled = pltpu.roll(c_ref[...], lb - dec, axis=1)
    out_ref[...] = rolled
    sub = lax.broadcasted_iota(jnp.int32, new_rows.shape, 0)
    shifted = jnp.where(sub >= SAMPLE_ROWS - dec,
                        pltpu.roll(new_rows, SAMPLE_ROWS - dec, axis=0), 0.0)
    padded = jnp.concatenate(
        [jnp.zeros((LANES - SAMPLE_ROWS, new_rows.shape[1]), F32), shifted], axis=0)
    new_t = padded.T
    lane_t = lax.broadcasted_iota(jnp.int32, new_t.shape, 1)
    out_ref[:, lb - LANES:] = jnp.where(lane_t >= LANES - dec, new_t, rolled[:, lb - LANES:])


def _ffn_kernel(*refs, final, dec, n_alias):
    x_ref, g_ref, wi_ref, wo_ref, fg_ref = refs[:5]
    refs = refs[5:]
    if dec:
        new_refs, cache_refs = refs[:N_GROUPS], refs[N_GROUPS:2 * N_GROUPS]
        refs = refs[2 * N_GROUPS + n_alias:]
        o_ref, roll_refs, refs = refs[0], refs[1:1 + N_GROUPS], refs[1 + N_GROUPS:]
        half_refs, refs = refs[:N_GROUPS], refs[N_GROUPS:]
    else:
        o_ref, refs = refs[0], refs[1:]
    h_s, acc_s = refs
    if dec:
        is_hi = (lax.broadcasted_iota(jnp.int32, (SAMPLE_ROWS, HALF_COLS), 0) * 0
                 + pl.program_id(0) % 2) == 1

        for g in range(N_GROUPS):
            kn, vn = (jnp.where(is_hi, new_refs[g][:, k * GROUP_COLS + HALF_COLS:(k + 1) * GROUP_COLS],
                                new_refs[g][:, k * GROUP_COLS:k * GROUP_COLS + HALF_COLS])
                      for k in (1, 2))
            _roll_window(cache_refs[g].at[0], kn, roll_refs[g].at[0], dec)
            _roll_window(cache_refs[g].at[1], vn, roll_refs[g].at[1], dec)
            half_refs[g][...] = cache_refs[g][...].astype(BF16)
    x = x_ref[...]
    h_s[...] = ((x * _rms_inv(x)) * g_ref[...]).astype(BF16)
    for c in range(N_FF_CHUNKS):
        lo = c * FF_CHUNK
        gate = jnp.dot(h_s[...], wi_ref[:, lo:lo + FF_CHUNK], preferred_element_type=F32)
        up = jnp.dot(h_s[...], wi_ref[:, D_FF + lo:D_FF + lo + FF_CHUNK],
                     preferred_element_type=F32)
        act = ((gate * jax.nn.sigmoid(gate)) * up).astype(BF16)
        part = jnp.dot(act, wo_ref[lo:lo + FF_CHUNK, :], preferred_element_type=F32)
        if c == 0:
            acc_s[...] = part
        else:
            acc_s[...] += part
    y = x_ref[...] + 0.5 * acc_s[...]
    if final:
        y = (y * _rms_inv(y)) * fg_ref[...]
    o_ref[...] = y


def _ffn_call(x, gains, wi, wo, final_gain, layer, tm, final, name, roll=None):
    n = x.shape[0]
    in_specs = [
        pl.BlockSpec((tm, D_MODEL), lambda i: (i, 0)),
        pl.BlockSpec((None, 1, D_MODEL), lambda i: (layer, 0, 0)),
        _resident((None, D_MODEL, 2 * D_FF), lambda i: (layer, 0, 0)),
        _resident((None, D_FF, D_MODEL), lambda i: (layer, 0, 0)),
        pl.BlockSpec((1, D_MODEL), lambda i: (0, 0)),
    ]
    args = [x, gains, wi, wo, final_gain]
    out_specs = [pl.BlockSpec((tm, D_MODEL), lambda i: (i, 0))]
    out_shape = [jax.ShapeDtypeStruct((n, D_MODEL), F32)]
    aliases, dec, n_alias = {}, 0, 0
    scratch = [pltpu.VMEM((tm, D_MODEL), BF16), pltpu.VMEM((tm, D_MODEL), F32)]
    if roll is not None:
        new_rows, caches, prev_outs, seq_base, dec = roll
        assert 2 * (caches[0].shape[1] // 2) == n // tm
        in_specs += [pl.BlockSpec((SAMPLE_ROWS, 3 * GROUP_COLS), lambda i: (seq_base + i // 2, 0))
                     for _ in new_rows]
        args += list(new_rows)
        slab_specs = [pl.BlockSpec((None, None, 2, HALF_COLS, c.shape[-1]),
                                   lambda i: (layer, seq_base + i // 2, 0, i % 2, 0)) for c in caches]
        in_specs += slab_specs
        args += list(caches)
        if prev_outs is not None:
            n_alias = N_GROUPS
            for g, prev in enumerate(prev_outs):
                aliases[len(args)] = 1 + g
                in_specs.append(pl.BlockSpec(memory_space=pl.ANY))
                args.append(prev)
        out_specs += slab_specs
        out_shape += [jax.ShapeDtypeStruct(c.shape, F32) for c in caches]
        out_specs += [pl.BlockSpec((None, 2, HALF_COLS, c.shape[-1]), lambda i: (i // 2, 0, i % 2, 0))
                      for c in caches]
        out_shape += [jax.ShapeDtypeStruct((n // tm // 2, 2, GROUP_COLS, c.shape[-1]), BF16)
                      for c in caches]
    return pl.pallas_call(
        functools.partial(_ffn_kernel, final=final, dec=dec, n_alias=n_alias),
        grid=(n // tm,),
        in_specs=in_specs,
        out_specs=out_specs,
        out_shape=out_shape,
        input_output_aliases=aliases,
        scratch_shapes=scratch,
        compiler_params=_params(1),
        name=name,
    )(*args)


def _qkv_kernel(x_ref, g_ref, w_ref, *refs, dils):
    n_g = len(dils)
    bf_refs, f32_refs = refs[:n_g], refs[n_g:2 * n_g]
    u_s, slab_s, gather_s, stage_s = refs[2 * n_g:2 * n_g + 4]
    perm_refs = list(refs[2 * n_g + 4:])
    tm = x_ref.shape[0]
    x = x_ref[...]
    u = (x * _rms_inv(x)) * g_ref[...]
    u_s[...] = u.astype(BF16)
    if perm_refs:
        for t in range(D_MODEL // LANES):
            slab_s[t] = u[:, t * LANES:(t + 1) * LANES]
    for g, dil in enumerate(dils):
        n = tm // dil
        lhs_ref = u_s
        if dil > 1:
            lhs_ref = perm_refs.pop()
            src = slab_s
            outer, blk = dil // FINE_STRIDE, tm // FINE_STRIDE
            if dil > FINE_STRIDE:
                src = gather_s
                for lo in range(FINE_STRIDE):
                    for t in range(D_MODEL // LANES):
                        gather_s[t, lo * blk:(lo + 1) * blk, :] = slab_s[t, pl.ds(lo, blk, stride=FINE_STRIDE), :]
            for r in range(dil):
                lo, hi = r % FINE_STRIDE, r // FINE_STRIDE
                rows = pl.ds(r, n, stride=dil) if dil <= FINE_STRIDE else pl.ds(lo * blk + hi, n, stride=outer)
                for t in range(D_MODEL // LANES):
                    lhs_ref[r * n:(r + 1) * n, t * LANES:(t + 1) * LANES] = src[t, rows, :].astype(BF16)
        even = (lax.broadcasted_iota(jnp.int32, (n, GROUP_COLS), 1) & (2 * HEAD_DIM - 1)) < HEAD_DIM
        for kind in range(3):
            col = (kind * n_g + g) * GROUP_COLS
            z = jnp.dot(lhs_ref[...], w_ref[:, col:col + GROUP_COLS], preferred_element_type=F32)
            if kind == 0:
                z = z * ATTN_SCALE
            first_kind = 3 - f32_refs[g].shape[0] // LANE_TILES
            if kind >= first_kind:
                base = (kind - first_kind) * LANE_TILES
                outer, blk = dil // FINE_STRIDE, tm // FINE_STRIDE
                for r in range(dil):
                    for t in range(LANE_TILES):
                        piece = z[r * n:(r + 1) * n, t * LANES:(t + 1) * LANES]
                        if dil <= FINE_STRIDE:
                            f32_refs[g][base + t, pl.ds(r, n, stride=dil), :] = piece
                        else:
                            lo, hi = r % FINE_STRIDE, r // FINE_STRIDE
                            stage_s[t, pl.ds(lo * blk + hi, n, stride=outer), :] = piece
                if dil > FINE_STRIDE:
                    for lo in range(FINE_STRIDE):
                        for t in range(LANE_TILES):
                            f32_refs[g][base + t, pl.ds(lo, blk, stride=FINE_STRIDE), :] = (
                                stage_s[t, lo * blk:(lo + 1) * blk, :])
            for r in range(dil):
                zr = z[r * n:(r + 1) * n]
                if kind == 0:
                    zq = zr * LOG2_E
                    bf_refs[g][r, :, :GROUP_COLS] = jnp.where(even, zq, 0.0).astype(BF16)
                    bf_refs[g][r, :, GROUP_COLS:2 * GROUP_COLS] = jnp.where(even, 0.0, zq).astype(BF16)
                else:
                    bf_refs[g][r, :, (kind + 1) * GROUP_COLS:(kind + 2) * GROUP_COLS] = zr.astype(BF16)


def _qkv_call(x, gains, w, layer, tm, seq_rows, tail_rows, dils, f32_parts, name):
    n = x.shape[0]
    tiles_per_seq = seq_rows // tm
    tail_tiles = tail_rows // tm
    n_seq = n // seq_rows

    def tail_map(i):
        return (i // tiles_per_seq, 0,
                jnp.maximum(i % tiles_per_seq - (tiles_per_seq - tail_tiles), 0), 0)

    specs = [pl.BlockSpec((None, dil, tm // dil, 4 * GROUP_COLS),
                          lambda i: (i // tiles_per_seq, 0, i % tiles_per_seq, 0)) for dil in dils]
    specs += [pl.BlockSpec((None, f32_parts * LANE_TILES, tm, LANES), tail_map) for _ in dils]
    shapes = [jax.ShapeDtypeStruct((n_seq, dil, seq_rows // dil, 4 * GROUP_COLS), BF16) for dil in dils]
    shapes += [jax.ShapeDtypeStruct((n_seq, f32_parts * LANE_TILES, tail_rows, LANES), F32)
               for _ in dils]
    n_perm = sum(dil > 1 for dil in dils)
    return pl.pallas_call(
        functools.partial(_qkv_kernel, dils=dils),
        grid=(n // tm,),
        in_specs=[
            pl.BlockSpec((tm, D_MODEL), lambda i: (i, 0)),
            pl.BlockSpec((None, 1, D_MODEL), lambda i: (layer, 0, 0)),
            _resident((None, D_MODEL, QKV_DIM), lambda i: (layer, 0, 0)),
        ],
        out_specs=specs,
        out_shape=shapes,
        scratch_shapes=[pltpu.VMEM((tm, D_MODEL), BF16),
                        pltpu.VMEM((D_MODEL // LANES, tm, LANES), F32),
                        pltpu.VMEM((D_MODEL // LANES, tm, LANES), F32),
                        pltpu.VMEM((LANE_TILES, tm, LANES), F32)]
                       + [pltpu.VMEM((tm, D_MODEL), BF16)] * n_perm,
        compiler_params=_params(1),
        name=name,
    )(x, gains, w)


CONV_CHUNK = 256
GATE_CHUNK = 512


def _mixer_kernel(*refs, sample, tiles_per_seq):
    x_ref, g_ref, wc_ref, wg_ref, bg_ref, cw_ref, wco_ref, wao_ref, wo_ref = refs[:9]
    refs = refs[9:]
    if sample:
        st_ref, refs = refs[0], refs[1:]
    attn_refs, refs = refs[:2 * N_GROUPS], refs[2 * N_GROUPS:]
    out_ref, p_ref, u_s, c_s, m_s, stage_s = refs[:6]
    scratch = list(refs[6:])
    if not sample:
        halo_s = scratch.pop(0)
    tm = x_ref.shape[0]
    x = x_ref[...]
    u_s[...] = ((x * _rms_inv(x)) * g_ref[...]).astype(BF16)
    row = lax.broadcasted_iota(jnp.int32, (tm, CONV_CHUNK), 0)

    if not sample:
        @pl.when(pl.program_id(0) % tiles_per_seq == 0)
        def _():
            halo_s[...] = jnp.zeros_like(halo_s)

    for c in range(CONV_DIM // CONV_CHUNK):
        sl = slice(c * CONV_CHUNK, (c + 1) * CONV_CHUNK)
        u = u_s[...]
        cb = jnp.dot(u, wc_ref[:, sl], preferred_element_type=F32)
        cc = jnp.dot(u, wc_ref[:, CONV_DIM + c * CONV_CHUNK:CONV_DIM + (c + 1) * CONV_CHUNK],
                     preferred_element_type=F32)
        ch = jnp.dot(u, wc_ref[:, 2 * CONV_DIM + c * CONV_CHUNK:2 * CONV_DIM + (c + 1) * CONV_CHUNK],
                     preferred_element_type=F32)
        p = cc * ch
        if sample:
            p_ref[:, sl] = p
            p = jnp.where((row & (SAMPLE_ROWS - 1)) >= SAMPLE_ROWS - 2, st_ref[:, sl], p)
            r1 = pltpu.roll(p, 1, axis=0)
            r2 = pltpu.roll(p, 2, axis=0)
        else:
            r1 = pltpu.roll(p, 1, axis=0)
            r2 = pltpu.roll(p, 2, axis=0)
            h = halo_s[:, sl]
            r1 = jnp.where(row == 0, h[7:8], r1)
            r2 = jnp.where(row == 0, h[6:7], jnp.where(row == 1, h[7:8], r2))
            halo_s[:, sl] = p[tm - 8:tm]
            p_ref[:, sl] = p[tm - 8:tm]
        conv = cw_ref[0:1, sl] * r2 + cw_ref[1:2, sl] * r1 + cw_ref[2:3, sl] * p
        c_s[:, sl] = (cb * conv).astype(BF16)

    def natural(ref):
        dil = ref.shape[0]
        if dil == 1:
            return ref[0]
        s_ref = scratch.pop()
        n, outer, blk = tm // dil, dil // FINE_STRIDE, tm // FINE_STRIDE
        for r in range(dil):
            for t in range(LANE_TILES):
                piece = ref[r, :, t * LANES:(t + 1) * LANES]
                if dil <= FINE_STRIDE:
                    s_ref[t, pl.ds(r, n, stride=dil), :] = piece
                else:
                    lo, hi = r % FINE_STRIDE, r // FINE_STRIDE
                    stage_s[t, pl.ds(lo * blk + hi, n, stride=outer), :] = piece
        if dil > FINE_STRIDE:
            for lo in range(FINE_STRIDE):
                for t in range(LANE_TILES):
                    s_ref[t, pl.ds(lo, blk, stride=FINE_STRIDE), :] = stage_s[t, lo * blk:(lo + 1) * blk, :]
        return jnp.concatenate([s_ref[t] for t in range(LANE_TILES)], axis=1)

    o = [natural(attn_refs[2 * g]) for g in range(N_GROUPS)]
    lse = [natural(attn_refs[2 * g + 1]) for g in range(N_GROUPS)]
    mx = jnp.maximum(jnp.maximum(lse[0], lse[1]), lse[2])
    e = [jnp.exp2(l - mx) for l in lse]
    a = (e[0] * o[0] + e[1] * o[1] + e[2] * o[2]) / (e[0] + e[1] + e[2])
    aproj = jnp.dot(a.astype(BF16), wao_ref[...], preferred_element_type=F32)
    cproj = jnp.dot(c_s[...], wco_ref[...], preferred_element_type=F32)
    for c in range(D_MODEL // GATE_CHUNK):
        sl = slice(c * GATE_CHUNK, (c + 1) * GATE_CHUNK)
        za = jnp.dot(u_s[...], wg_ref[:, sl], preferred_element_type=F32) + bg_ref[:, sl]
        zc = (jnp.dot(u_s[...], wg_ref[:, D_MODEL + c * GATE_CHUNK:D_MODEL + (c + 1) * GATE_CHUNK],
                      preferred_element_type=F32)
              + bg_ref[:, D_MODEL + c * GATE_CHUNK:D_MODEL + (c + 1) * GATE_CHUNK])
        m = jax.nn.sigmoid(za) * aproj[:, sl] + jax.nn.sigmoid(zc) * cproj[:, sl]
        m_s[:, sl] = m.astype(BF16)
    out_ref[...] = x_ref[...] + jnp.dot(m_s[...], wo_ref[...], preferred_element_type=F32)


def _mixer_call(x, gains, wc, wg, bg, cw, wco, wao, wo, state, attn, layer, tm, seq_rows, name):
    n = x.shape[0]
    sample = state is not None
    tiles_per_seq = seq_rows // tm
    row_spec = pl.BlockSpec((tm, D_MODEL), lambda i: (i, 0))
    in_specs = [
        row_spec,
        pl.BlockSpec((None, 1, D_MODEL), lambda i: (layer, 0, 0)),
        _resident((None, D_MODEL, 3 * CONV_DIM), lambda i: (layer, 0, 0)),
        _resident((None, D_MODEL, 2 * D_MODEL), lambda i: (layer, 0, 0)),
        pl.BlockSpec((None, 1, 2 * D_MODEL), lambda i: (layer, 0, 0)),
        pl.BlockSpec((None, 3, CONV_DIM), lambda i: (layer, 0, 0)),
        _resident((None, CONV_DIM, D_MODEL), lambda i: (layer, 0, 0)),
        _resident((None, GROUP_COLS, D_MODEL), lambda i: (layer, 0, 0)),
        _resident((None, D_MODEL, D_MODEL), lambda i: (layer, 0, 0)),
    ]
    args = [x, gains, wc, wg, bg, cw, wco, wao, wo]
    scratch = [pltpu.VMEM((tm, D_MODEL), BF16), pltpu.VMEM((tm, CONV_DIM), BF16),
               pltpu.VMEM((tm, D_MODEL), BF16), pltpu.VMEM((LANE_TILES, tm, LANES), F32)]
    if sample:
        in_specs.append(row_spec)
        args.append(state)
        p_spec = row_spec
        p_shape = jax.ShapeDtypeStruct((n, CONV_DIM), F32)
    else:
        p_spec = pl.BlockSpec((8, CONV_DIM), lambda i: (i // tiles_per_seq, 0))
        p_shape = jax.ShapeDtypeStruct((n // seq_rows * 8, CONV_DIM), F32)
        scratch.append(pltpu.VMEM((8, CONV_DIM), F32))
    flat = [a for pair in attn for a in pair]
    in_specs += [pl.BlockSpec((None, a.shape[1], tm // a.shape[1], GROUP_COLS),
                              lambda i: (i // tiles_per_seq, 0, i % tiles_per_seq, 0))
                 for a in flat]
    scratch += [pltpu.VMEM((LANE_TILES, tm, LANES), F32)] * sum(a.shape[1] > 1 for a in flat)
    return pl.pallas_call(
        functools.partial(_mixer_kernel, sample=sample, tiles_per_seq=tiles_per_seq),
        grid=(n // tm,),
        in_specs=in_specs,
        out_specs=[row_spec, p_spec],
        out_shape=[jax.ShapeDtypeStruct((n, D_MODEL), F32), p_shape],
        scratch_shapes=scratch,
        compiler_params=_params(1),
        name=name,
    )(*args, *flat)


def _attn_prompt_kernel(qe_ref, qo_ref, kp_ref, kc_ref, vp_ref, vc_ref, o_ref, l_ref):
    n = pl.program_id(2)
    qi = lax.broadcasted_iota(jnp.int32, (QBLOCK, 2 * QBLOCK), 0)
    col = lax.broadcasted_iota(jnp.int32, (QBLOCK, 2 * QBLOCK), 1)
    cur_ok = (col >= QBLOCK) & (col - QBLOCK <= qi)
    bias_inner = jnp.where(cur_ok | ((col < QBLOCK) & (col >= qi)), 0.0, NEG)
    bias_first = jnp.where(cur_ok | ((col < QBLOCK) & (col >= qi + jnp.where(n > 0, 0, QBLOCK))),
                           0.0, NEG)
    even = lax.broadcasted_iota(jnp.int32, (QBLOCK, 2 * HEAD_DIM), 1) < HEAD_DIM
    streams, step_rows, _ = qe_ref.shape
    for st, i in [(st, i) for st in range(streams) for i in range(step_rows // QBLOCK)]:
        rows = slice(i * QBLOCK, (i + 1) * QBLOCK)
        if i == 0:
            k = jnp.concatenate([kp_ref[st], kc_ref[st, rows]], axis=0)
            v = jnp.concatenate([vp_ref[st], vc_ref[st, rows]], axis=0)
            bias = bias_first
        else:
            k = kc_ref[st, (i - 1) * QBLOCK:(i + 1) * QBLOCK]
            v = vc_ref[st, (i - 1) * QBLOCK:(i + 1) * QBLOCK]
            bias = bias_inner
        for j in range(HEADS // 2):
            sl = slice(2 * HEAD_DIM * j, 2 * HEAD_DIM * (j + 1))
            halves = []
            for q_ref in (qe_ref, qo_ref):
                s = lax.dot_general(q_ref[st, rows, sl], k[:, sl], (((1,), (1,)), ((), ())),
                                    preferred_element_type=F32) + bias
                m = jnp.max(s, axis=1, keepdims=True)
                p = jnp.exp2(s - m)
                l = jnp.sum(p, axis=1, keepdims=True)
                pv = jnp.dot(p.astype(BF16), v[:, sl], preferred_element_type=F32)
                halves.append((pv / l, jnp.broadcast_to(m + jnp.log2(l), pv.shape)))
            o_ref[st, rows, sl] = jnp.where(even, halves[0][0], halves[1][0])
            l_ref[st, rows, sl] = jnp.where(even, halves[0][1], halves[1][1])


def _attn_prompt_call(qkv, name):
    batch, dil, length, _ = qkv.shape
    blocks = min(MAX_ATTN_BLOCKS, length // QBLOCK)
    streams = min(dil, MAX_ATTN_BLOCKS // blocks)
    step_rows = blocks * QBLOCK
    nb = length // step_rows
    assert length % step_rows == 0 and dil % streams == 0

    def spec(part, prev=False):
        if prev:
            return pl.BlockSpec((None, streams, QBLOCK, GROUP_COLS),
                                lambda b, r, n: (b, r, jnp.maximum(blocks * n - 1, 0), part))
        return pl.BlockSpec((None, streams, step_rows, GROUP_COLS), lambda b, r, n: (b, r, n, part))

    out_spec = pl.BlockSpec((None, streams, step_rows, GROUP_COLS), lambda b, r, n: (b, r, n, 0))
    out_shape = jax.ShapeDtypeStruct((batch, dil, length, GROUP_COLS), F32)
    return pl.pallas_call(
        _attn_prompt_kernel,
        grid=(batch, dil // streams, nb),
        in_specs=[spec(0), spec(1), spec(2, True), spec(2), spec(3, True), spec(3)],
        out_specs=[out_spec, out_spec],
        out_shape=[out_shape, out_shape],
        compiler_params=_params(3),
        name=name,
    )(qkv, qkv, qkv, qkv, qkv, qkv)


def _sample_attn(q, kn, vn, kt, vt, o_ref, l_ref, dil, dec):
    width, lb = kt.shape
    sub = lax.broadcasted_iota(jnp.int32, (SAMPLE_ROWS, width), 0)
    lane = lax.broadcasted_iota(jnp.int32, (SAMPLE_ROWS, width), 1)
    diag = sub == lane // HEAD_DIM
    qb = jnp.concatenate(
        [jnp.where(diag, jnp.broadcast_to(q[t:t + 1], diag.shape), 0.0) for t in range(dec)],
        axis=0).astype(BF16)
    rows = dec * SAMPLE_ROWS
    s = jnp.dot(qb, kt, preferred_element_type=F32)
    t_of = lax.broadcasted_iota(jnp.int32, (rows, lb), 0) // SAMPLE_ROWS
    diff = lax.broadcasted_iota(jnp.int32, (rows, lb), 1) - t_of
    valid = (diff >= 0) & ((diff & (dil - 1)) == 0)
    s = jnp.where(valid, s, NEG)
    m = jnp.max(s, axis=1, keepdims=True)

    qbf = qb.astype(F32)
    knr = kn.astype(BF16).astype(F32)
    vnr = vn.astype(BF16).astype(F32)
    t_col = lax.broadcasted_iota(jnp.int32, (rows, 1), 0) // SAMPLE_ROWS
    s_new = []
    for t2 in range(dec):
        s2 = jnp.sum(qbf * knr[t2:t2 + 1], axis=1, keepdims=True)
        d2 = t_col - t2
        s2 = jnp.where((d2 >= 0) & ((d2 & (dil - 1)) == 0), s2, NEG)
        s_new.append(s2)
        m = jnp.maximum(m, s2)
    p = jnp.exp(s - m)
    l = jnp.sum(p, axis=1, keepdims=True)
    acc = lax.dot_general(p.astype(BF16), vt, (((1,), (1,)), ((), ())),
                          preferred_element_type=F32)
    for t2 in range(dec):
        pn = jnp.exp(s_new[t2] - m)
        l = l + pn
        acc = acc + pn * vnr[t2:t2 + 1]
    o = acc / l
    lse = jnp.broadcast_to((m + jnp.log(l)) * LOG2_E, o.shape)
    o_rows, l_rows = [], []
    for t in range(dec):
        blk = slice(t * SAMPLE_ROWS, (t + 1) * SAMPLE_ROWS)
        o_rows.append(jnp.sum(jnp.where(diag, o[blk], 0.0), axis=0, keepdims=True))
        l_rows.append(jnp.sum(jnp.where(diag, lse[blk], 0.0), axis=0, keepdims=True))
    pad = jnp.zeros((SAMPLE_ROWS - dec, width), F32)
    o_ref[...] = jnp.concatenate(o_rows + [pad], axis=0)
    l_ref[...] = jnp.concatenate(l_rows + [pad], axis=0)


def _attn_sample_kernel(*refs, dils, dec):
    n_g = len(dils)
    for j in range(refs[n_g].shape[0]):
        rows = slice(j * SAMPLE_ROWS, (j + 1) * SAMPLE_ROWS)
        for g, dil in enumerate(dils):
            new_ref, c_ref = refs[g], refs[n_g + g]
            o_ref, l_ref = refs[2 * n_g + 2 * g:2 * n_g + 2 * g + 2]
            q, kn, vn = (new_ref[rows, k * GROUP_COLS:(k + 1) * GROUP_COLS] for k in range(3))
            _sample_attn(q, kn, vn, c_ref[j, 0], c_ref[j, 1], o_ref.at[rows], l_ref.at[rows], dil, dec)


def _attn_sample_call(new_rows, windows, seq_base, dils, dec, name):
    seqs = windows[0].shape[0]
    per = SAMPLE_SEQS_PER_STEP
    assert seqs % per == 0 and seq_base % per == 0
    in_specs = [pl.BlockSpec((per * SAMPLE_ROWS, 3 * GROUP_COLS), lambda s: (seq_base // per + s, 0))
                for _ in new_rows]
    in_specs += [pl.BlockSpec((per, 2, GROUP_COLS, w.shape[-1]), lambda s: (s, 0, 0, 0))
                 for w in windows]
    out_row = pl.BlockSpec((per * SAMPLE_ROWS, GROUP_COLS), lambda s: (s, 0))
    return pl.pallas_call(
        functools.partial(_attn_sample_kernel, dils=dils, dec=dec),
        grid=(seqs // per,),
        in_specs=in_specs,
        out_specs=[out_row] * (2 * len(dils)),
        out_shape=[jax.ShapeDtypeStruct((seqs * SAMPLE_ROWS, GROUP_COLS), F32)] * (2 * len(dils)),
        compiler_params=_params(1),
        name=name,
    )(*new_rows, *windows)


def kernel(x_prompt, x_sample, cache_kv1, cache_kv2, cache_kv3, state_conv, ffn1_norm, ffn1_w_in,
           ffn1_w_out, mix_norm, w_in, b_gate, conv_w, w_attn_out, w_conv_out, w_out, ffn2_norm,
           ffn2_w_in, ffn2_w_out, final_norm):
    batch, seq, _ = x_prompt.shape
    n_seq, dec, _ = x_sample.shape
    depth = w_in.shape[0]
    caches = (cache_kv1, cache_kv2, cache_kv3)
    assert dec <= SAMPLE_ROWS - 2 and seq % (GROUPS[-1][1] * QBLOCK) == 0
    for cache, (win, dil) in zip(caches, GROUPS):
        assert cache.shape[2] == win and win == SPAN * dil and (dil == 1 or dec <= dil)

    f1_wi, f1_wo = ffn1_w_in.astype(BF16), ffn1_w_out.astype(BF16)
    f2_wi, f2_wo = ffn2_w_in.astype(BF16), ffn2_w_out.astype(BF16)
    w_qkv, w_conv, w_gate = _cast_split_call(w_in, (QKV_DIM, 3 * CONV_DIM, 2 * D_MODEL), LANES,
                                             "cast_w_in")
    w_ao = w_attn_out.astype(BF16)
    w_co = w_conv_out.astype(BF16)
    w_o = w_out.astype(BF16)
    g1 = ffn1_norm.reshape(depth, 1, D_MODEL)
    gm = mix_norm.reshape(depth, 1, D_MODEL)
    g2 = ffn2_norm.reshape(depth, 1, D_MODEL)
    gf = final_norm.reshape(1, D_MODEL)
    bg = b_gate.reshape(depth, 1, 2 * D_MODEL)

    caches_t = [c.transpose(0, 1, 3, 4, 5, 2).reshape(depth, n_seq, 2, GROUP_COLS, c.shape[2])
                for c in caches]

    xp = x_prompt.reshape(batch * seq, D_MODEL)
    xs = jnp.pad(x_sample, ((0, 0), (0, SAMPLE_ROWS - dec), (0, 0))).reshape(n_seq * SAMPLE_ROWS, D_MODEL)
    ns = xs.shape[0]
    tail_rows = GROUPS[-1][0]
    dils = tuple(dil for _, dil in GROUPS)

    kv_prompt = [[] for _ in GROUPS]
    conv_prompt, conv_sample = [], []
    new_caches = None
    for l in range(depth):
        last = l == depth - 1
        xs = _ffn_call(xs, g1, f1_wi, f1_wo, gf, l, ns, False, f"s_ffn1_{l}")[0]
        s_qkv = _qkv_call(xs, gm, w_qkv, l, ns, ns, ns, (1,) * N_GROUPS, 3, f"s_qkv_{l}")[N_GROUPS:]
        s_tail = [a[0].transpose(1, 0, 2).reshape(ns, 3 * GROUP_COLS) for a in s_qkv]

        outs = _ffn_call(xp, g1, f1_wi, f1_wo, gf, l, ROW_TILE, False, f"p_ffn1_{l}",
                         roll=(s_tail, caches_t, new_caches, 0, dec))
        xp, new_caches, windows_a = outs[0], outs[1:1 + N_GROUPS], outs[1 + N_GROUPS:]
        p_qkv = _qkv_call(xp, gm, w_qkv, l, ROW_TILE, seq, tail_rows, dils, 2, f"p_qkv_{l}")
        p_attn = []
        for g, (win, dil) in enumerate(GROUPS):
            p_attn.append(_attn_prompt_call(p_qkv[g], f"p_attn{g}_{l}"))
            keep = min(win, seq)
            kv = p_qkv[N_GROUPS + g][:, :, tail_rows - keep:]
            kv = kv.reshape(batch, 2, LANE_TILES, keep, 2, HEAD_DIM).transpose(0, 3, 1, 2, 4, 5)
            kv_prompt[g].append(kv.reshape(batch, keep, 2, HEADS, HEAD_DIM))
        xp, p_p = _mixer_call(xp, gm, w_conv, w_gate, bg, conv_w, w_co, w_ao, w_o, None, p_attn, l,
                              ROW_TILE, seq, f"p_mix_{l}")
        conv_prompt.append(p_p.reshape(batch, 8, CONV_DIM)[:, 6:8])
        outs = _ffn_call(xp, g2, f2_wi, f2_wo, gf, l, ROW_TILE, last, f"p_ffn2_{l}",
                         roll=(s_tail, caches_t, new_caches, n_seq // 2, dec))
        xp, new_caches, windows_b = outs[0], outs[1:1 + N_GROUPS], outs[1 + N_GROUPS:]

        attn_a = _attn_sample_call(s_tail, windows_a, 0, dils, dec, f"s_attn_a_{l}")
        attn_b = _attn_sample_call(s_tail, windows_b, n_seq // 2, dils, dec, f"s_attn_b_{l}")
        s_attn = [tuple(jnp.concatenate([attn_a[2 * g + j], attn_b[2 * g + j]], axis=0)
                        .reshape(1, 1, ns, GROUP_COLS) for j in range(2)) for g in range(N_GROUPS)]
        st = jnp.pad(state_conv[l], ((0, 0), (SAMPLE_ROWS - 2, 0), (0, 0)))
        st = jnp.roll(st, -1, axis=0).reshape(ns, CONV_DIM)
        xs, s_p = _mixer_call(xs, gm, w_conv, w_gate, bg, conv_w, w_co, w_ao, w_o, st, s_attn, l,
                              ns, ns, f"s_mix_{l}")
        conv_sample.append(s_p.reshape(n_seq, SAMPLE_ROWS, CONV_DIM)[:, dec - 2:dec])
        xs = _ffn_call(xs, g2, f2_wi, f2_wo, gf, l, ns, last, f"s_ffn2_{l}")[0]

    y_prompt = xp.reshape(batch, seq, D_MODEL)
    y_sample = xs.reshape(n_seq, SAMPLE_ROWS, D_MODEL)[:, :dec]
    kv_sample = [c.reshape(depth, n_seq, 2, HEADS, HEAD_DIM, c.shape[-1]).transpose(0, 1, 5, 2, 3, 4)
                 for c in new_caches]
    return (y_prompt, y_sample,
            jnp.stack(kv_prompt[0]), jnp.stack(kv_prompt[1]), jnp.stack(kv_prompt[2]),
            jnp.stack(conv_prompt),
            kv_sample[0], kv_sample[1], kv_sample[2],
            jnp.stack(conv_sample))
```
